```python
import math
import jax, jax.numpy as jnp
from jax import lax
import numpy as np

D_MODEL = 2048
BATCH = 1
SEQ = 16384
DEPTH = 4

N_MIXERS = 4
N_HEADS = 16
HEAD_DIM = D_MODEL // N_HEADS
N_ATT_HEADS = 8
ATT_HEAD_DIM = D_MODEL // N_ATT_HEADS
D_FF = 4 * D_MODEL
CHUNK = 16
Q_BLOCK = 128
CONV_WIDTH = 3
LN_EPS = 1e-5
RMS_EPS = 1e-6
ALPHA = (2.0 * DEPTH) ** 0.25
BETA = (8.0 * DEPTH) ** -0.25
N_LAYERS_A = (DEPTH + 3) // 4
N_LAYERS_B = (DEPTH + 2) // 4
N_LAYERS_C = (DEPTH + 1) // 4
N_LAYERS_D = DEPTH // 4

kernel_name = "hybrid_hgrn2_stickbreak_fox_shortconv"


def _heads(t, n_heads):
    b, s, _ = t.shape
    return t.reshape(b, s, n_heads, -1).transpose(0, 2, 1, 3)


def _merge(t):
    b, h, s, d = t.shape
    return t.transpose(0, 2, 1, 3).reshape(b, s, h * d)


def layer_norm(x, g, b):
    xf = x.astype(jnp.float32)
    mu = jnp.mean(xf, axis=-1, keepdims=True)
    var = jnp.mean(jnp.square(xf - mu), axis=-1, keepdims=True)
    return ((xf - mu) * lax.rsqrt(var + LN_EPS) * g + b).astype(x.dtype)


def _block_mask(i, strict):
    q_pos = i * Q_BLOCK + jnp.arange(Q_BLOCK)
    k_pos = jnp.arange((i + 1) * Q_BLOCK).reshape(i + 1, Q_BLOCK)
    if strict:
        return k_pos[None] < q_pos[:, None, None]
    return k_pos[None] <= q_pos[:, None, None]


def gated_linear_recurrence(q, k, v, log_f):
    bsz, h, s, dk = q.shape
    dv = v.shape[-1]
    n = s // CHUNK

    def to_chunks(t):
        return jnp.moveaxis(t.astype(jnp.float32).reshape(bsz, h, n, CHUNK, t.shape[-1]), 2, 0)

    qc, kc, vc, gc = to_chunks(q), to_chunks(k), to_chunks(v), to_chunks(log_f)
    bc = jnp.cumsum(gc, axis=-2)
    b_last = bc[..., -1:, :]
    q_dec = qc * jnp.exp(bc)
    k_inv = kc * jnp.exp(-bc)
    k_dec = kc * jnp.exp(b_last - bc)
    causal = jnp.tril(jnp.ones((CHUNK, CHUNK), dtype=bool))
    scores = jnp.where(causal, jnp.einsum('nbhtd,nbhsd->nbhts', q_dec, k_inv), 0.0)
    intra = jnp.einsum('nbhts,nbhsv->nbhtv', scores, vc)

    def step(state, inp):
        qd, kd, vb, dl = inp
        inter = jnp.einsum('bhtd,bhdv->bhtv', qd, state)
        new_state = dl[..., 0, :, None] * state + jnp.einsum('bhsd,bhsv->bhdv', kd, vb)
        return new_state, inter

    state0 = jnp.zeros((bsz, h, dk, dv), jnp.float32)
    _, inter = lax.scan(step, state0, (q_dec, k_dec, vc, jnp.exp(b_last)))
    return jnp.moveaxis(intra + inter, 0, 2).reshape(bsz, h, s, dv)


def hgrn2_mixer(x, w_in, norm_g, w_out, lb):
    q, f, i, g = jnp.split(x @ w_in, 4, axis=-1)
    q = jax.nn.silu(q.astype(jnp.float32)) * HEAD_DIM ** -0.5
    f = f.astype(jnp.float32)
    log_f = jnp.logaddexp(jnp.log(lb), jnp.log1p(-lb) + jax.nn.log_sigmoid(f))
    k = (1.0 - lb) * jax.nn.sigmoid(-f)
    o = gated_linear_recurrence(_heads(q, N_HEADS), _heads(k, N_HEADS),
                                _heads(i, N_HEADS), _heads(log_f, N_HEADS))
    o = o * lax.rsqrt(jnp.mean(jnp.square(o), axis=-1, keepdims=True) + RMS_EPS) * norm_g
    o = _merge(o) * jax.nn.silu(g.astype(jnp.float32))
    return o.astype(x.dtype) @ w_out


def stick_breaking_mixer(x, w_in, w_out):
    q, k, v = [_heads(t, N_ATT_HEADS) for t in jnp.split(x @ w_in, 3, axis=-1)]
    bsz, h, s, d = q.shape
    nb = s // Q_BLOCK
    q = q * d ** -0.5
    k_blk = k.reshape(bsz, h, nb, Q_BLOCK, d)
    v_blk = v.reshape(bsz, h, nb, Q_BLOCK, d)
    incl_rev = jnp.tril(jnp.ones((Q_BLOCK, Q_BLOCK), jnp.float32))
    outs = []
    for i in range(nb):
        qb = q[:, :, i * Q_BLOCK:(i + 1) * Q_BLOCK]
        kp, vp = k_blk[:, :, :i + 1], v_blk[:, :, :i + 1]
        z = jnp.einsum('bhtd,bhnsd->bhtns', qb, kp).astype(jnp.float32)
        z = jnp.where(_block_mask(i, strict=True), z, -jnp.inf)
        sp = jax.nn.relu(z) + jnp.log1p(jnp.exp(-jnp.abs(z)))
        within = jnp.einsum('bhtnj,js->bhtns', sp, incl_rev)
        later_blocks = jnp.tril(jnp.ones((i + 1, i + 1), jnp.float32), -1)
        offs = jnp.einsum('bhtm,mn->bhtn', within[..., 0], later_blocks)
        w = jnp.exp(z - (within + offs[..., None]))
        outs.append(jnp.einsum('bhtns,bhnsd->bhtd', w.astype(vp.dtype), vp))
    out = jnp.concatenate(outs, axis=2)
    return _merge(out) @ w_out


def forgetting_mixer(x, w_in, b_f, w_out):
    proj = x @ w_in
    q, k, v = [_heads(t, N_ATT_HEADS) for t in jnp.split(proj[..., :3 * D_MODEL], 3, axis=-1)]
    log_f = jax.nn.log_sigmoid((proj[..., 3 * D_MODEL:] + b_f).astype(jnp.float32))
    cum = jnp.cumsum(log_f, axis=1).transpose(0, 2, 1)
    bsz, h, s, d = q.shape
    nb = s // Q_BLOCK
    q = q * d ** -0.5
    k_blk = k.reshape(bsz, h, nb, Q_BLOCK, d)
    v_blk = v.reshape(bsz, h, nb, Q_BLOCK, d)
    cum_blk = cum.reshape(bsz, h, nb, Q_BLOCK)
    outs = []
    for i in range(nb):
        qb = q[:, :, i * Q_BLOCK:(i + 1) * Q_BLOCK]
        kp, vp = k_blk[:, :, :i + 1], v_blk[:, :, :i + 1]
        cq = cum_blk[:, :, i]
        scores = jnp.einsum('bhtd,bhnsd->bhtns', qb, kp).astype(jnp.float32)
        bias = cq[:, :, :, None, None] - cum_blk[:, :, None, :i + 1, :]
        logits = jnp.where(_block_mask(i, strict=False), scores + bias, -jnp.inf)
        e = jnp.exp(logits - jnp.max(logits, axis=(3, 4), keepdims=True))
        denom = jnp.sum(e, axis=(3, 4))
        o = jnp.einsum('bhtns,bhnsd->bhtd', e.astype(vp.dtype), vp)
        outs.append((o / denom[..., None]).astype(vp.dtype))
    out = jnp.concatenate(outs, axis=2)
    return _merge(out) @ w_out


def short_conv_mixer(x, w_in, conv_w, w_out):
    b_gate, c_gate, hid = jnp.split(x @ w_in, 3, axis=-1)
    u = c_gate * hid
    y = lax.conv_general_dilated(
        u, conv_w.astype(u.dtype)[:, None, :], window_strides=(1,),
        padding=[(CONV_WIDTH - 1, 0)], dimension_numbers=('NWC', 'WIO', 'NWC'),
        feature_group_count=D_MODEL)
    return (b_gate * y) @ w_out


def squared_relu_mlp(x, w1, w2):
    return jnp.square(jax.nn.relu(x @ w1)) @ w2


def setup_inputs(seed: int = 0) -> dict:
    key = jax.random.key(seed)
    ks = jax.random.split(key, 24)
    d = D_MODEL

    def nrm(k, shape, scale):
        return jax.random.normal(k, shape, jnp.float32) * scale

    return {
        "x": nrm(ks[0], (BATCH, SEQ, d), 1.0),
        "w_mix_a": nrm(ks[1], (N_LAYERS_A, d, 4 * d), d ** -0.5),
        "norm_g_a": 1.0 + nrm(ks[2], (N_LAYERS_A, HEAD_DIM), 0.02),
        "lb_logits": nrm(ks[3], (DEPTH + 1, d), 0.5),
        "w_out_a": nrm(ks[4], (N_LAYERS_A, d, d), BETA * d ** -0.5),
        "w_mix_b": nrm(ks[5], (N_LAYERS_B, d, 3 * d), d ** -0.5),
        "w_out_b": nrm(ks[6], (N_LAYERS_B, d, d), BETA * d ** -0.5),
        "w_mix_c": nrm(ks[7], (N_LAYERS_C, d, 3 * d + N_ATT_HEADS), d ** -0.5),
        "b_f_c": nrm(ks[8], (N_LAYERS_C, N_ATT_HEADS), 0.1),
        "w_out_c": nrm(ks[9], (N_LAYERS_C, d, d), BETA * d ** -0.5),
        "w_mix_d": nrm(ks[10], (N_LAYERS_D, d, 3 * d), d ** -0.5),
        "conv_w_d": nrm(ks[11], (N_LAYERS_D, CONV_WIDTH, d), CONV_WIDTH ** -0.5),
        "w_out_d": nrm(ks[12], (N_LAYERS_D, d, d), BETA * d ** -0.5),
        "ln_mix_g": 1.0 + nrm(ks[13], (DEPTH, d), 0.02),
        "ln_mix_b": nrm(ks[14], (DEPTH, d), 0.02),
        "w_ff1": nrm(ks[15], (DEPTH, d, D_FF), d ** -0.5),
        "w_ff2": nrm(ks[16], (DEPTH, D_FF, d), BETA * D_FF ** -0.5),
        "ln_ff_g": 1.0 + nrm(ks[17], (DEPTH, d), 0.02),
        "ln_ff_b": nrm(ks[18], (DEPTH, d), 0.02),
    }


def reference(x, w_mix_a, norm_g_a, lb_logits, w_out_a, w_mix_b, w_out_b,
              w_mix_c, b_f_c, w_out_c, w_mix_d, conv_w_d, w_out_d,
              ln_mix_g, ln_mix_b, w_ff1, w_ff2, ln_ff_g, ln_ff_b):
    lb_table = jnp.cumsum(jax.nn.softmax(lb_logits.astype(jnp.float32), axis=0), axis=0)
    h = x
    for i in range(DEPTH):
        m, j = i % N_MIXERS, i // N_MIXERS
        if m == 0:
            y = hgrn2_mixer(h, w_mix_a[j], norm_g_a[j], w_out_a[j], lb_table[i])
        elif m == 1:
            y = stick_breaking_mixer(h, w_mix_b[j], w_out_b[j])
        elif m == 2:
            y = forgetting_mixer(h, w_mix_c[j], b_f_c[j], w_out_c[j])
        else:
            y = short_conv_mixer(h, w_mix_d[j], conv_w_d[j], w_out_d[j])
        h = layer_norm(ALPHA * h + y, ln_mix_g[i], ln_mix_b[i])
        h = layer_norm(ALPHA * h + squared_relu_mlp(h, w_ff1[i], w_ff2[i]), ln_ff_g[i], ln_ff_b[i])
    return h
```

```python
import functools

import jax
import jax.numpy as jnp
from jax import lax
from jax.experimental import pallas as pl
from jax.experimental.pallas import tpu as pltpu

DEPTH = 4
N_MIXERS = 4
HGRN_HEAD_DIM = 128
ATT_HEAD_DIM = 256
CHUNK = 16
CONV_WIDTH = 3
LN_EPS = 1e-5
RMS_EPS = 1e-6
ALPHA = (2.0 * DEPTH) ** 0.25
MASK_VALUE = -1e30

V7X_VMEM_LIMIT_BYTES = 56 * 1024 * 1024
SUBLANES = 8

F32 = jnp.float32
BF16 = jnp.bfloat16


def _params(n_grid_axes, vmem_bytes=V7X_VMEM_LIMIT_BYTES):
    return pltpu.CompilerParams(
        dimension_semantics=("arbitrary",) * n_grid_axes,
        vmem_limit_bytes=vmem_bytes)


def _layer_norm(v, g, b):
    mu = jnp.mean(v, axis=-1, keepdims=True)
    c = v - mu
    var = jnp.mean(c * c, axis=-1, keepdims=True)
    return c * lax.rsqrt(var + LN_EPS) * g + b


def _split3(x):
    hi = x.astype(BF16)
    r1 = x - hi.astype(F32)
    mid = r1.astype(BF16)
    lo = (r1 - mid.astype(F32)).astype(BF16)
    return hi, mid, lo


def _dot3(mat01, x):
    hi, mid, lo = _split3(x)
    out = jnp.dot(mat01, lo, preferred_element_type=F32)
    out += jnp.dot(mat01, mid, preferred_element_type=F32)
    out += jnp.dot(mat01, hi, preferred_element_type=F32)
    return out


def _softplus(z):
    return jnp.maximum(z, 0.0) + jnp.log1p(jnp.exp(-jnp.abs(z)))


def _proj_body(x_ref, w_ref, o_ref):
    o_ref[...] = jnp.dot(x_ref[...], w_ref[...],
                         preferred_element_type=F32).astype(o_ref.dtype)


def _proj(x, w, out_dtype, tm, tn):
    m, k = x.shape
    n = w.shape[1]
    tm, tn = min(tm, m), min(tn, n)
    assert m % tm == 0 and n % tn == 0, (m, n, tm, tn)
    return pl.pallas_call(
        _proj_body,
        out_shape=jax.ShapeDtypeStruct((m, n), out_dtype),
        grid=(m // tm, n // tn),
        in_specs=[pl.BlockSpec((tm, k), lambda i, j: (i, 0)),
                  pl.BlockSpec((k, tn), lambda i, j: (0, j))],
        out_specs=pl.BlockSpec((tm, tn), lambda i, j: (i, j)),
        compiler_params=_params(2),
        name="proj",
    )(x, w)


def _outproj_ln_body(y_ref, w_ref, h_ref, g_ref, b_ref, of_ref, ob_ref):
    acc = jnp.dot(y_ref[...], w_ref[...], preferred_element_type=F32)
    out = _layer_norm(ALPHA * h_ref[...] + acc, g_ref[...], b_ref[...])
    of_ref[...] = out
    ob_ref[...] = out.astype(BF16)


def _outproj_ln(y, w, h, g, b, tm):
    m, d = h.shape
    tm = min(tm, m)
    row = lambda i: (i, 0)
    fixed = lambda i: (0, 0)
    return pl.pallas_call(
        _outproj_ln_body,
        out_shape=(jax.ShapeDtypeStruct((m, d), F32),
                   jax.ShapeDtypeStruct((m, d), BF16)),
        grid=(m // tm,),
        in_specs=[pl.BlockSpec((tm, d), row),
                  pl.BlockSpec((d, d), fixed),
                  pl.BlockSpec((tm, d), row),
                  pl.BlockSpec((1, d), fixed),
                  pl.BlockSpec((1, d), fixed)],
        out_specs=(pl.BlockSpec((tm, d), row), pl.BlockSpec((tm, d), row)),
        compiler_params=_params(1),
        name="outproj_ln",
    )(y, w, h, g.reshape(1, d), b.reshape(1, d))


def _mlp_ln_body(xb_ref, h_ref, w1_ref, w2_ref, g_ref, b_ref, of_ref, ob_ref, acc_ref):
    j = pl.program_id(1)

    @pl.when(j == 0)
    def _():
        acc_ref[...] = ALPHA * h_ref[...]

    a = jnp.dot(xb_ref[...], w1_ref[...], preferred_element_type=F32)
    a = jnp.maximum(a, 0.0)
    acc_ref[...] += jnp.dot((a * a).astype(BF16), w2_ref[...], preferred_element_type=F32)

    @pl.when(j == pl.num_programs(1) - 1)
    def _():
        out = _layer_norm(acc_ref[...], g_ref[...], b_ref[...])
        of_ref[...] = out
        ob_ref[...] = out.astype(BF16)


def _mlp_ln(xb, h, w1, w2, g, b, tm, tf):
    m, d = h.shape
    f = w1.shape[1]
    tm, tf = min(tm, m), min(tf, f)
    row = lambda i, j: (i, 0)
    fixed = lambda i, j: (0, 0)
    return pl.pallas_call(
        _mlp_ln_body,
        out_shape=(jax.ShapeDtypeStruct((m, d), F32),
                   jax.ShapeDtypeStruct((m, d), BF16)),
        grid=(m // tm, f // tf),
        in_specs=[pl.BlockSpec((tm, d), row),
                  pl.BlockSpec((tm, d), row),
                  pl.BlockSpec((d, tf), lambda i, j: (0, j)),
                  pl.BlockSpec((tf, d), lambda i, j: (j, 0)),
                  pl.BlockSpec((1, d), fixed),
                  pl.BlockSpec((1, d), fixed)],
        out_specs=(pl.BlockSpec((tm, d), row), pl.BlockSpec((tm, d), row)),
        scratch_shapes=[pltpu.VMEM((tm, d), F32)],
        compiler_params=_params(2),
        name="mlp_ln",
    )(xb, h, w1, w2, g.reshape(1, d), b.reshape(1, d))


def _hgrn_body(layer, n_heads, q_ref, f_ref, i_ref, g_ref, lbl_ref, ng_ref, o_ref,
               st_ref, qd_ref, kd_ref, v_ref, dl_ref, acc_ref):
    t = q_ref.shape[0]
    dk = HGRN_HEAD_DIM

    @pl.when(pl.program_id(1) == 0)
    def _():
        st_ref[...] = jnp.zeros_like(st_ref)

    lg = lbl_ref[...]
    e = jnp.exp(lg - jnp.max(lg, axis=0, keepdims=True))
    lb = jnp.sum(e[:layer + 1], axis=0, keepdims=True) / jnp.sum(e, axis=0, keepdims=True)

    f = f_ref[...]
    log_sig = jnp.minimum(f, 0.0) - jnp.log1p(jnp.exp(-jnp.abs(f)))
    a = jnp.log(lb)
    b = jnp.log1p(-lb) + log_sig
    log_f = jnp.maximum(a, b) + jnp.log1p(jnp.exp(-jnp.abs(a - b)))
    k = (1.0 - lb) * jax.nn.sigmoid(-f)
    q = q_ref[...]
    q = q * jax.nn.sigmoid(q) * dk ** -0.5

    r = lax.broadcasted_iota(jnp.int32, (t, t), 0)
    c = lax.broadcasted_iota(jnp.int32, (t, t), 1)
    same_chunk = (r // CHUNK) == (c // CHUNK)
    causal = same_chunk & (c <= r)
    tri = jnp.where(causal, 1.0, 0.0).astype(BF16)
    ones = jnp.where(same_chunk, 1.0, 0.0).astype(BF16)
    bc = _dot3(tri, log_f)
    b_last = _dot3(ones, log_f)

    qd_ref[...] = (q * jnp.exp(bc)).astype(BF16)
    k_inv = (k * jnp.exp(-bc)).astype(BF16)
    kd_ref[...] = (k * jnp.exp(b_last - bc)).astype(BF16)
    dl_ref[...] = jnp.exp(b_last)
    v_ref[...] = i_ref[...].astype(BF16)

    for h in range(n_heads):
        cols = slice(h * dk, (h + 1) * dk)
        sc = lax.dot_general(qd_ref[:, cols], k_inv[:, cols], (((1,), (1,)), ((), ())),
                             preferred_element_type=F32)
        sc = jnp.where(causal, sc, 0.0).astype(BF16)
        acc_ref[:, cols] = jnp.dot(sc, v_ref[:, cols], preferred_element_type=F32)

    def chunk_step(ci, carry):
        rows = pl.ds(pl.multiple_of(ci * CHUNK, CHUNK), CHUNK)
        for h in range(n_heads):
            cols = slice(h * dk, (h + 1) * dk)
            st = st_ref[h]
            inter = lax.dot_general(qd_ref[rows, cols], st.astype(BF16),
                                    (((1,), (1,)), ((), ())), preferred_element_type=F32)
            acc_ref[rows, cols] += inter
            upd = lax.dot_general(v_ref[rows, cols], kd_ref[rows, cols],
                                  (((0,), (0,)), ((), ())), preferred_element_type=F32)
            decay = dl_ref[pl.ds(ci * CHUNK, 1), cols]
            st_ref[h] = decay * st + upd
        return carry

    lax.fori_loop(0, t // CHUNK, chunk_step, 0)

    ng = ng_ref[...]
    g = g_ref[...]
    gate = g * jax.nn.sigmoid(g)
    for h in range(n_heads):
        cols = slice(h * dk, (h + 1) * dk)
        o = acc_ref[:, cols]
        o = o * lax.rsqrt(jnp.mean(o * o, axis=-1, keepdims=True) + RMS_EPS) * ng
        o_ref[:, cols] = (o * gate[:, cols]).astype(BF16)


def _hgrn(proj, lb_logits, norm_g, layer, t_block, heads_per_step):
    s, d4 = proj.shape
    d = d4 // 4
    t_block = min(t_block, s)
    gw = heads_per_step * HGRN_HEAD_DIM
    n_groups = d // gw
    n_lb = lb_logits.shape[0]
    col = lambda off: (lambda hg, tb: (tb, off * n_groups + hg))
    return pl.pallas_call(
        functools.partial(_hgrn_body, layer, heads_per_step),
        out_shape=jax.ShapeDtypeStruct((s, d), BF16),
        grid=(n_groups, s // t_block),
        in_specs=[pl.BlockSpec((t_block, gw), col(0)),
                  pl.BlockSpec((t_block, gw), col(1)),
                  pl.BlockSpec((t_block, gw), col(2)),
                  pl.BlockSpec((t_block, gw), col(3)),
                  pl.BlockSpec((n_lb, gw), lambda hg, tb: (0, hg)),
                  pl.BlockSpec((1, HGRN_HEAD_DIM), lambda hg, tb: (0, 0))],
        out_specs=pl.BlockSpec((t_block, gw), lambda hg, tb: (tb, hg)),
        scratch_shapes=[pltpu.VMEM((heads_per_step, HGRN_HEAD_DIM, HGRN_HEAD_DIM), F32),
                        pltpu.VMEM((t_block, gw), BF16),
                        pltpu.VMEM((t_block, gw), BF16),
                        pltpu.VMEM((t_block, gw), BF16),
                        pltpu.VMEM((t_block, gw), F32),
                        pltpu.VMEM((t_block, gw), F32)],
        compiler_params=_params(2),
        name="hgrn2",
    )(proj, proj, proj, proj, lb_logits, norm_g.reshape(1, HGRN_HEAD_DIM))


def _stickbreak_body(q_ref, k_ref, v_ref, o_ref, acc_ref):
    t, d = q_ref.shape
    i = pl.program_id(1)
    q = q_ref[...] * d ** -0.5
    r = lax.broadcasted_iota(jnp.int32, (t, t), 0)
    c = lax.broadcasted_iota(jnp.int32, (t, t), 1)
    incl_rev = jnp.where(r >= c, 1.0, 0.0).astype(BF16)
    strict = c < r

    def tile(j, offs, diagonal):
        rows = pl.ds(pl.multiple_of(j * t, t), t)
        z = lax.dot_general(q, k_ref[rows, :], (((1,), (1,)), ((), ())),
                            preferred_element_type=F32)
        if diagonal:
            z = jnp.where(strict, z, MASK_VALUE)
        sp = _softplus(z)
        within = jnp.dot(sp.astype(BF16), incl_rev, preferred_element_type=F32)
        w = jnp.exp(z - (within + offs))
        acc_ref[...] += jnp.dot(w.astype(BF16), v_ref[rows, :], preferred_element_type=F32)
        return offs + within[:, 0:1]

    acc_ref[...] = jnp.zeros_like(acc_ref)
    offs = tile(i, jnp.zeros((t, 1), F32), True)
    lax.fori_loop(0, i, lambda n, offs: tile(i - 1 - n, offs, False), offs)
    o_ref[...] = acc_ref[...].astype(o_ref.dtype)


def _stickbreak(qkv, n_heads, t_block):
    s = qkv.shape[0]
    d = ATT_HEAD_DIM
    t_block = min(t_block, s)
    return pl.pallas_call(
        _stickbreak_body,
        out_shape=jax.ShapeDtypeStruct((s, n_heads * d), BF16),
        grid=(n_heads, s // t_block),
        in_specs=[pl.BlockSpec((t_block, d), lambda h, i: (i, h)),
                  pl.BlockSpec((s, d), lambda h, i: (0, n_heads + h)),
                  pl.BlockSpec((s, d), lambda h, i: (0, 2 * n_heads + h))],
        out_specs=pl.BlockSpec((t_block, d), lambda h, i: (i, h)),
        scratch_shapes=[pltpu.VMEM((t_block, d), F32)],
        compiler_params=_params(2),
        name="stickbreak_attn",
    )(qkv, qkv, qkv)


def _fox_gate_body(fg_ref, bf_ref, cum_ref, carry_ref):
    t = fg_ref.shape[0]

    @pl.when(pl.program_id(0) == 0)
    def _():
        carry_ref[...] = jnp.zeros_like(carry_ref)

    x = fg_ref[...] + bf_ref[...]
    log_f = jnp.minimum(x, 0.0) - jnp.log1p(jnp.exp(-jnp.abs(x)))
    r = lax.broadcasted_iota(jnp.int32, (t, t), 0)
    c = lax.broadcasted_iota(jnp.int32, (t, t), 1)
    tri = jnp.where(c <= r, 1.0, 0.0).astype(BF16)
    cum = _dot3(tri, log_f) + carry_ref[...]
    cum_ref[...] = cum
    carry_ref[...] = cum[t - 1:t, :]


def _fox_gates(fg, bf, t_block):
    s, w = fg.shape
    t_block = min(t_block, s)
    return pl.pallas_call(
        _fox_gate_body,
        out_shape=jax.ShapeDtypeStruct((s, w), F32),
        grid=(s // t_block,),
        in_specs=[pl.BlockSpec((t_block, w), lambda i: (i, 0)),
                  pl.BlockSpec((1, w), lambda i: (0, 0))],
        out_specs=pl.BlockSpec((t_block, w), lambda i: (i, 0)),
        scratch_shapes=[pltpu.VMEM((1, w), F32)],
        compiler_params=_params(1),
        name="fox_gates",
    )(fg, bf)


def _fox_body(q_ref, k_ref, v_ref, cq_ref, ck_ref, o_ref, acc_ref):
    t, d = q_ref.shape
    i = pl.program_id(1)
    q = q_ref[...] * d ** -0.5
    cq = cq_ref[...]
    r = lax.broadcasted_iota(jnp.int32, (t, t), 0)
    c = lax.broadcasted_iota(jnp.int32, (t, t), 1)
    causal = c <= r

    def tile(j, carry, diagonal):
        m, l = carry
        rows = pl.ds(pl.multiple_of(j * t, t), t)
        z = lax.dot_general(q, k_ref[rows, :], (((1,), (1,)), ((), ())),
                            preferred_element_type=F32)
        z = z + (cq - ck_ref[:, rows])
        if diagonal:
            z = jnp.where(causal, z, MASK_VALUE)
        m_new = jnp.maximum(m, jnp.max(z, axis=-1, keepdims=True))
        scale = jnp.exp(m - m_new)
        p = jnp.exp(z - m_new)
        l = scale * l + jnp.sum(p, axis=-1, keepdims=True)
        acc_ref[...] = scale * acc_ref[...] + jnp.dot(
            p.astype(BF16), v_ref[rows, :], preferred_element_type=F32)
        return m_new, l

    acc_ref[...] = jnp.zeros_like(acc_ref)
    init = (jnp.full((t, 1), MASK_VALUE, F32), jnp.zeros((t, 1), F32))
    carry = lax.fori_loop(0, i, lambda j, carry: tile(j, carry, False), init)
    _, l = tile(i, carry, True)
    o_ref[...] = (acc_ref[...] / l).astype(o_ref.dtype)


def _fox(qkv, cum_col, cum_row, n_heads, t_block):
    s = qkv.shape[0]
    d = ATT_HEAD_DIM
    t_block = min(t_block, s)
    return pl.pallas_call(
        _fox_body,
        out_shape=jax.ShapeDtypeStruct((s, n_heads * d), BF16),
        grid=(n_heads, s // t_block),
        in_specs=[pl.BlockSpec((t_block, d), lambda h, i: (i, h)),
                  pl.BlockSpec((s, d), lambda h, i: (0, n_heads + h)),
                  pl.BlockSpec((s, d), lambda h, i: (0, 2 * n_heads + h)),
                  pl.BlockSpec((None, t_block, 1), lambda h, i: (h, i, 0)),
                  pl.BlockSpec((None, 1, s), lambda h, i: (h, 0, 0))],
        out_specs=pl.BlockSpec((t_block, d), lambda h, i: (i, h)),
        scratch_shapes=[pltpu.VMEM((t_block, d), F32)],
        compiler_params=_params(2),
        name="fox_attn",
    )(qkv, qkv, qkv, cum_col, cum_row)


def _conv_outproj_ln_body(bg_ref, cg_ref, hd_ref, cgp_ref, hdp_ref, cw_ref, w_ref, h_ref,
                          g_ref, b_ref, of_ref, ob_ref):
    i = pl.program_id(0)
    u = cg_ref[...] * hd_ref[...]
    u_prev = jnp.where(i > 0, cgp_ref[...] * hdp_ref[...], 0.0)
    row8 = lax.broadcasted_iota(jnp.int32, u_prev.shape, 0)
    y = cw_ref[CONV_WIDTH - 1:CONV_WIDTH, :] * u
    for lag in range(1, CONV_WIDTH):
        shifted = pltpu.roll(u, lag, 0)
        head = jnp.where(row8 < lag, pltpu.roll(u_prev, lag, 0), shifted[:SUBLANES])
        shifted = jnp.concatenate([head, shifted[SUBLANES:]], axis=0)
        y += cw_ref[CONV_WIDTH - 1 - lag:CONV_WIDTH - lag, :] * shifted
    acc = jnp.dot((bg_ref[...] * y).astype(BF16), w_ref[...], preferred_element_type=F32)
    out = _layer_norm(ALPHA * h_ref[...] + acc, g_ref[...], b_ref[...])
    of_ref[...] = out
    ob_ref[...] = out.astype(BF16)


def _conv_outproj_ln(proj, conv_w, w, h, g, b, tm):
    m, d = h.shape
    tm = min(tm, m)
    blocks_per_tile = tm // SUBLANES
    col = lambda off: (lambda i: (i, off))
    prev = lambda off: (lambda i: (jnp.maximum(i * blocks_per_tile - 1, 0), off))
    row = lambda i: (i, 0)
    fixed = lambda i: (0, 0)
    return pl.pallas_call(
        _conv_outproj_ln_body,
        out_shape=(jax.ShapeDtypeStruct((m, d), F32),
                   jax.ShapeDtypeStruct((m, d), BF16)),
        grid=(m // tm,),
        in_specs=[pl.BlockSpec((tm, d), col(0)),
                  pl.BlockSpec((tm, d), col(1)),
                  pl.BlockSpec((tm, d), col(2)),
                  pl.BlockSpec((SUBLANES, d), prev(1)),
                  pl.BlockSpec((SUBLANES, d), prev(2)),
                  pl.BlockSpec((CONV_WIDTH, d), fixed),
                  pl.BlockSpec((d, d), fixed),
                  pl.BlockSpec((tm, d), row),
                  pl.BlockSpec((1, d), fixed),
                  pl.BlockSpec((1, d), fixed)],
        out_specs=(pl.BlockSpec((tm, d), row), pl.BlockSpec((tm, d), row)),
        compiler_params=_params(1),
        name="conv_outproj_ln",
    )(proj, proj, proj, proj, proj, conv_w, w, h, g.reshape(1, d), b.reshape(1, d))


def _tiles(s):
    return dict(
        proj_tm=min(1024, s), proj_tn=1024,
        out_tm=min(512, s), conv_tm=min(256, s),
        mlp_tm=min(512, s), mlp_tf=512,
        hgrn_t=min(256, s), hgrn_heads=4,
        att_t=min(256, s),
        gate_t=min(256, s),
    )


def kernel(x, w_mix_a, norm_g_a, lb_logits, w_out_a, w_mix_b, w_out_b, w_mix_c, b_f_c,
           w_out_c, w_mix_d, conv_w_d, w_out_d, ln_mix_g, ln_mix_b, w_ff1, w_ff2,
           ln_ff_g, ln_ff_b):
    bsz, s, d = x.shape
    n_att_heads = d // ATT_HEAD_DIM
    tl = _tiles(s)
    outs = []
    for bi in range(bsz):
        h = x[bi]
        hb = h.astype(BF16)
        for i in range(DEPTH):
            m, j = i % N_MIXERS, i // N_MIXERS
            g, b = ln_mix_g[i], ln_mix_b[i]
            if m == 0:
                proj = _proj(hb, w_mix_a[j].astype(BF16), F32, tl["proj_tm"], tl["proj_tn"])
                y = _hgrn(proj, lb_logits, norm_g_a[j], i, tl["hgrn_t"], tl["hgrn_heads"])
                h, hb = _outproj_ln(y, w_out_a[j].astype(BF16), h, g, b, tl["out_tm"])
            elif m == 1:
                qkv = _proj(hb, w_mix_b[j].astype(BF16), BF16, tl["proj_tm"], tl["proj_tn"])
                y = _stickbreak(qkv, n_att_heads, tl["att_t"])
                h, hb = _outproj_ln(y, w_out_b[j].astype(BF16), h, g, b, tl["out_tm"])
            elif m == 2:
                wc = w_mix_c[j]
                qkv = _proj(hb, wc[:, :3 * d].astype(BF16), BF16, tl["proj_tm"], tl["proj_tn"])
                lane_pad = HGRN_HEAD_DIM - n_att_heads
                wf = jnp.pad(wc[:, 3 * d:], ((0, 0), (0, lane_pad))).astype(BF16)
                fg = _proj(hb, wf, F32, tl["proj_tm"], tl["proj_tn"])
                bf = jnp.pad(b_f_c[j], (0, lane_pad)).reshape(1, -1)
                cum = _fox_gates(fg, bf, tl["gate_t"])[:, :n_att_heads]
                cum_t = cum.T
                y = _fox(qkv, cum_t[:, :, None], cum_t[:, None, :], n_att_heads, tl["att_t"])
                h, hb = _outproj_ln(y, w_out_c[j].astype(BF16), h, g, b, tl["out_tm"])
            else:
                proj = _proj(hb, w_mix_d[j].astype(BF16), F32, tl["proj_tm"], tl["proj_tn"])
                h, hb = _conv_outproj_ln(proj, conv_w_d[j], w_out_d[j].astype(BF16), h, g, b,
                                         tl["conv_tm"])
            h, hb = _mlp_ln(hb, h, w_ff1[i].astype(BF16), w_ff2[i].astype(BF16),
                            ln_ff_g[i], ln_ff_b[i], tl["mlp_tm"], tl["mlp_tf"])
        outs.append(h)
    return jnp.stack(outs, axis=0)
```

```python
import functools

import jax
import jax.numpy as jnp
from jax import lax
from jax.experimental import pallas as pl
from jax.experimental.pallas import tpu as pltpu

DEPTH = 4
N_MIXERS = 4
HGRN_HEAD_DIM = 128
ATT_HEAD_DIM = 256
CHUNK = 16
CONV_WIDTH = 3
LN_EPS = 1e-5
RMS_EPS = 1e-6
ALPHA = (2.0 * DEPTH) ** 0.25
MASK_VALUE = -1e30
LOG2E = 1.4426950408889634
V7X_MXU_WIDTH = 256
SB_CUMSUM_WIDTH = V7X_MXU_WIDTH

V7X_VMEM_LIMIT_BYTES = 56 * 1024 * 1024
SUBLANES = 8

F32 = jnp.float32
BF16 = jnp.bfloat16


def _params(n_grid_axes, vmem_bytes=V7X_VMEM_LIMIT_BYTES):
    return pltpu.CompilerParams(
        dimension_semantics=("arbitrary",) * n_grid_axes,
        vmem_limit_bytes=vmem_bytes)


def _layer_norm(v, g, b):
    mu = jnp.mean(v, axis=-1, keepdims=True)
    c = v - mu
    var = jnp.mean(c * c, axis=-1, keepdims=True)
    return c * lax.rsqrt(var + LN_EPS) * g + b


def _split3(x):
    hi = x.astype(BF16)
    r1 = x - hi.astype(F32)
    mid = r1.astype(BF16)
    lo = (r1 - mid.astype(F32)).astype(BF16)
    return hi, mid, lo


def _dot3(mat01, x):
    hi, mid, lo = _split3(x)
    out = jnp.dot(mat01, lo, preferred_element_type=F32)
    out += jnp.dot(mat01, mid, preferred_element_type=F32)
    out += jnp.dot(mat01, hi, preferred_element_type=F32)
    return out


def _softplus(z):
    return jnp.maximum(z, 0.0) + jnp.log1p(jnp.exp(-jnp.abs(z)))


def _proj_body(x_ref, w_ref, o_ref):
    o_ref[...] = jnp.dot(x_ref[...], w_ref[...],
                         preferred_element_type=F32).astype(o_ref.dtype)


def _proj(x, w, out_dtype, tm, tn):
    m, k = x.shape
    n = w.shape[1]
    tm, tn = min(tm, m), min(tn, n)
    assert m % tm == 0 and n % tn == 0, (m, n, tm, tn)
    return pl.pallas_call(
        _proj_body,
        out_shape=jax.ShapeDtypeStruct((m, n), out_dtype),
        grid=(m // tm, n // tn),
        in_specs=[pl.BlockSpec((tm, k), lambda i, j: (i, 0)),
                  pl.BlockSpec((k, tn), lambda i, j: (0, j))],
        out_specs=pl.BlockSpec((tm, tn), lambda i, j: (i, j)),
        compiler_params=_params(2),
        name="proj",
    )(x, w)


def _outproj_ln_body(y_ref, w_ref, h_ref, g_ref, b_ref, of_ref, ob_ref):
    acc = jnp.dot(y_ref[...], w_ref[...], preferred_element_type=F32)
    out = _layer_norm(ALPHA * h_ref[...] + acc, g_ref[...], b_ref[...])
    of_ref[...] = out
    ob_ref[...] = out.astype(BF16)


def _outproj_ln(y, w, h, g, b, tm):
    m, d = h.shape
    tm = min(tm, m)
    row = lambda i: (i, 0)
    fixed = lambda i: (0, 0)
    return pl.pallas_call(
        _outproj_ln_body,
        out_shape=(jax.ShapeDtypeStruct((m, d), F32),
                   jax.ShapeDtypeStruct((m, d), BF16)),
        grid=(m // tm,),
        in_specs=[pl.BlockSpec((tm, d), row),
                  pl.BlockSpec((d, d), fixed),
                  pl.BlockSpec((tm, d), row),
                  pl.BlockSpec((1, d), fixed),
                  pl.BlockSpec((1, d), fixed)],
        out_specs=(pl.BlockSpec((tm, d), row), pl.BlockSpec((tm, d), row)),
        compiler_params=_params(1),
        name="outproj_ln",
    )(y, w, h, g.reshape(1, d), b.reshape(1, d))


def _mlp_ln_body(xb_ref, h_ref, w1_ref, w2_ref, g_ref, b_ref, of_ref, ob_ref):
    j = pl.program_id(1)

    @pl.when(j == 0)
    def _():
        of_ref[...] = ALPHA * h_ref[...]

    a = jnp.dot(xb_ref[...], w1_ref[...], preferred_element_type=F32)
    a = jnp.maximum(a, 0.0)
    of_ref[...] += jnp.dot((a * a).astype(BF16), w2_ref[...], preferred_element_type=F32)

    @pl.when(j == pl.num_programs(1) - 1)
    def _():
        out = _layer_norm(of_ref[...], g_ref[...], b_ref[...])
        of_ref[...] = out
        ob_ref[...] = out.astype(BF16)


def _mlp_ln(xb, h, w1, w2, g, b, tm, tf):
    m, d = h.shape
    f = w1.shape[1]
    tm, tf = min(tm, m), min(tf, f)
    row = lambda i, j: (i, 0)
    fixed = lambda i, j: (0, 0)
    return pl.pallas_call(
        _mlp_ln_body,
        out_shape=(jax.ShapeDtypeStruct((m, d), F32),
                   jax.ShapeDtypeStruct((m, d), BF16)),
        grid=(m // tm, f // tf),
        in_specs=[pl.BlockSpec((tm, d), row),
                  pl.BlockSpec((tm, d), row, pipeline_mode=pl.Buffered(1)),
                  pl.BlockSpec((d, tf), lambda i, j: (0, j)),
                  pl.BlockSpec((tf, d), lambda i, j: (j, 0)),
                  pl.BlockSpec((1, d), fixed),
                  pl.BlockSpec((1, d), fixed)],
        out_specs=(pl.BlockSpec((tm, d), row),
                   pl.BlockSpec((tm, d), row, pipeline_mode=pl.Buffered(1))),
        compiler_params=_params(2),
        name="mlp_ln",
    )(xb, h, w1, w2, g.reshape(1, d), b.reshape(1, d))


def _hgrn_body(layer, n_heads, q_ref, f_ref, i_ref, g_ref, lbl_ref, ng_ref, o_ref,
               st_ref, qd_ref, kd_ref, v_ref, dl_ref, acc_ref):
    t = q_ref.shape[0]
    dk = HGRN_HEAD_DIM

    @pl.when(pl.program_id(1) == 0)
    def _():
        st_ref[...] = jnp.zeros_like(st_ref)

    lg = lbl_ref[...]
    e = jnp.exp(lg - jnp.max(lg, axis=0, keepdims=True))
    lb = jnp.sum(e[:layer + 1], axis=0, keepdims=True) / jnp.sum(e, axis=0, keepdims=True)

    f = f_ref[...]
    log_sig = jnp.minimum(f, 0.0) - jnp.log1p(jnp.exp(-jnp.abs(f)))
    a = jnp.log(lb)
    b = jnp.log1p(-lb) + log_sig
    log_f = jnp.maximum(a, b) + jnp.log1p(jnp.exp(-jnp.abs(a - b)))
    k = (1.0 - lb) * jax.nn.sigmoid(-f)
    q = q_ref[...]
    q = q * jax.nn.sigmoid(q) * dk ** -0.5

    r = lax.broadcasted_iota(jnp.int32, (t, t), 0)
    c = lax.broadcasted_iota(jnp.int32, (t, t), 1)
    same_chunk = (r // CHUNK) == (c // CHUNK)
    causal = same_chunk & (c <= r)
    tri = jnp.where(causal, 1.0, 0.0).astype(BF16)
    ones = jnp.where(same_chunk, 1.0, 0.0).astype(BF16)
    bc = _dot3(tri, log_f)
    b_last = _dot3(ones, log_f)

    qd_ref[...] = (q * jnp.exp(bc)).astype(BF16)
    k_inv = (k * jnp.exp(-bc)).astype(BF16)
    kd_ref[...] = k * jnp.exp(b_last - bc)
    dl_ref[...] = jnp.exp(b_last)
    v_ref[...] = i_ref[...].astype(BF16)

    for h in range(n_heads):
        cols = slice(h * dk, (h + 1) * dk)
        sc = lax.dot_general(qd_ref[:, cols], k_inv[:, cols], (((1,), (1,)), ((), ())),
                             preferred_element_type=F32)
        sc = jnp.where(causal, sc, 0.0).astype(BF16)
        acc_ref[:, cols] = jnp.dot(sc, v_ref[:, cols], preferred_element_type=F32)

    n_chunks = t // CHUNK
    chunk_of_row = lax.broadcasted_iota(jnp.int32, (t, dk), 0) // CHUNK
    for h in range(n_heads):
        cols = slice(h * dk, (h + 1) * dk)
        kd = kd_ref[:, cols]
        kd_by_chunk = jnp.concatenate(
            [jnp.where(chunk_of_row == ci, kd, 0.0) for ci in range(n_chunks)], axis=1)
        upd = lax.dot_general(v_ref[:, cols], kd_by_chunk.astype(BF16),
                              (((0,), (0,)), ((), ())), preferred_element_type=F32)
        st = st_ref[h]
        for ci in range(n_chunks):
            rows = slice(ci * CHUNK, (ci + 1) * CHUNK)
            inter = lax.dot_general(qd_ref[rows, cols], st.astype(BF16),
                                    (((1,), (1,)), ((), ())), preferred_element_type=F32)
            acc_ref[rows, cols] += inter
            st = dl_ref[ci * CHUNK:ci * CHUNK + 1, cols] * st + upd[:, ci * dk:(ci + 1) * dk]
        st_ref[h] = st

    ng = ng_ref[...]
    g = g_ref[...]
    gate = g * jax.nn.sigmoid(g)
    for h in range(n_heads):
        cols = slice(h * dk, (h + 1) * dk)
        o = acc_ref[:, cols]
        o = o * lax.rsqrt(jnp.mean(o * o, axis=-1, keepdims=True) + RMS_EPS) * ng
        o_ref[:, cols] = (o * gate[:, cols]).astype(BF16)


def _hgrn(proj, lb_logits, norm_g, layer, t_block, heads_per_step):
    s, d4 = proj.shape
    d = d4 // 4
    t_block = min(t_block, s)
    gw = heads_per_step * HGRN_HEAD_DIM
    n_groups = d // gw
    n_lb = lb_logits.shape[0]
    col = lambda off: (lambda hg, tb: (tb, off * n_groups + hg))
    return pl.pallas_call(
        functools.partial(_hgrn_body, layer, heads_per_step),
        out_shape=jax.ShapeDtypeStruct((s, d), BF16),
        grid=(n_groups, s // t_block),
        in_specs=[pl.BlockSpec((t_block, gw), col(0)),
                  pl.BlockSpec((t_block, gw), col(1)),
                  pl.BlockSpec((t_block, gw), col(2)),
                  pl.BlockSpec((t_block, gw), col(3)),
                  pl.BlockSpec((n_lb, gw), lambda hg, tb: (0, hg)),
                  pl.BlockSpec((1, HGRN_HEAD_DIM), lambda hg, tb: (0, 0))],
        out_specs=pl.BlockSpec((t_block, gw), lambda hg, tb: (tb, hg)),
        scratch_shapes=[pltpu.VMEM((heads_per_step, HGRN_HEAD_DIM, HGRN_HEAD_DIM), F32),
                        pltpu.VMEM((t_block, gw), BF16),
                        pltpu.VMEM((t_block, gw), F32),
                        pltpu.VMEM((t_block, gw), BF16),
                        pltpu.VMEM((t_block, gw), F32),
                        pltpu.VMEM((t_block, gw), F32)],
        compiler_params=_params(2),
        name="hgrn2",
    )(proj, proj, proj, proj, lb_logits, norm_g.reshape(1, HGRN_HEAD_DIM))


def _stickbreak_body(q_ref, k_ref, v_ref, o_ref, acc_ref, lw_ref):
    t, d = q_ref.shape
    cw = min(SB_CUMSUM_WIDTH, t)
    i = pl.program_id(1)
    q = q_ref[...] * d ** -0.5
    r = lax.broadcasted_iota(jnp.int32, (cw, cw), 0)
    c = lax.broadcasted_iota(jnp.int32, (cw, cw), 1)
    incl_rev = jnp.where(r >= c, 1.0, 0.0).astype(BF16)

    def log_weights(j, diagonal):
        rows = pl.ds(pl.multiple_of(j * t, t), t)
        z = lax.dot_general(q, k_ref[rows, :], (((1,), (1,)), ((), ())),
                            preferred_element_type=F32)
        if diagonal:
            qpos = lax.broadcasted_iota(jnp.int32, (t, t), 0)
            kpos = lax.broadcasted_iota(jnp.int32, (t, t), 1)
            z = jnp.where(kpos < qpos, z, MASK_VALUE)
        sp = jnp.maximum(z, 0.0) + jnp.log(1.0 + jnp.exp2(jnp.abs(z) * -LOG2E))
        sp = sp.astype(BF16)
        later = None
        for sub in reversed(range(t // cw)):
            cols = slice(sub * cw, (sub + 1) * cw)
            within = jnp.dot(sp[:, cols], incl_rev, preferred_element_type=F32)
            total = within if later is None else within + later
            lw_ref[:, cols] = z[:, cols] - total
            later = total[:, 0:1]
        return later

    def accumulate(j, offs):
        rows = pl.ds(pl.multiple_of(j * t, t), t)
        w = jnp.exp(lw_ref[...] - offs)
        acc_ref[...] += jnp.dot(w.astype(BF16), v_ref[rows, :], preferred_element_type=F32)

    def step(n, carry):
        offs, block_sum = carry
        j = i - n
        accumulate(j, offs)
        return offs + block_sum, log_weights(j - 1, False)

    acc_ref[...] = jnp.zeros_like(acc_ref)
    carry = (jnp.zeros((t, 1), F32), log_weights(i, True))
    offs, _ = lax.fori_loop(0, i, step, carry)
    accumulate(0, offs)
    o_ref[...] = acc_ref[...].astype(o_ref.dtype)


def _stickbreak(qkv, n_heads, t_block):
    s = qkv.shape[0]
    d = ATT_HEAD_DIM
    t_block = min(t_block, s)
    assert s % t_block == 0
    return pl.pallas_call(
        _stickbreak_body,
        out_shape=jax.ShapeDtypeStruct((s, n_heads * d), BF16),
        grid=(n_heads, s // t_block),
        in_specs=[pl.BlockSpec((t_block, d), lambda h, i: (i, h)),
                  pl.BlockSpec((s, d), lambda h, i: (0, n_heads + h)),
                  pl.BlockSpec((s, d), lambda h, i: (0, 2 * n_heads + h))],
        out_specs=pl.BlockSpec((t_block, d), lambda h, i: (i, h)),
        scratch_shapes=[pltpu.VMEM((t_block, d), F32),
                        pltpu.VMEM((t_block, t_block), F32)],
        compiler_params=_params(2),
        name="stickbreak_attn",
    )(qkv, qkv, qkv)


def _fox_gate_body(fg_ref, bf_ref, cum_ref, carry_ref):
    t = fg_ref.shape[0]

    @pl.when(pl.program_id(0) == 0)
    def _():
        carry_ref[...] = jnp.zeros_like(carry_ref)

    x = fg_ref[...] + bf_ref[...]
    log_f = jnp.minimum(x, 0.0) - jnp.log1p(jnp.exp(-jnp.abs(x)))
    r = lax.broadcasted_iota(jnp.int32, (t, t), 0)
    c = lax.broadcasted_iota(jnp.int32, (t, t), 1)
    tri = jnp.where(c <= r, 1.0, 0.0).astype(BF16)
    cum = _dot3(tri, log_f) + carry_ref[...]
    cum_ref[...] = cum
    carry_ref[...] = cum[t - 1:t, :]


def _fox_gates(fg, bf, t_block):
    s, w = fg.shape
    t_block = min(t_block, s)
    return pl.pallas_call(
        _fox_gate_body,
        out_shape=jax.ShapeDtypeStruct((s, w), F32),
        grid=(s // t_block,),
        in_specs=[pl.BlockSpec((t_block, w), lambda i: (i, 0)),
                  pl.BlockSpec((1, w), lambda i: (0, 0))],
        out_specs=pl.BlockSpec((t_block, w), lambda i: (i, 0)),
        scratch_shapes=[pltpu.VMEM((1, w), F32)],
        compiler_params=_params(1),
        name="fox_gates",
    )(fg, bf)


def _fox_body(q_ref, k_ref, v_ref, cq_ref, ck_ref, o_ref, acc_ref, z_ref):
    t, d = q_ref.shape
    i = pl.program_id(1)
    q = q_ref[...] * d ** -0.5
    cq = cq_ref[...]

    def logits(j, diagonal):
        rows = pl.ds(pl.multiple_of(j * t, t), t)
        z = lax.dot_general(q, k_ref[rows, :], (((1,), (1,)), ((), ())),
                            preferred_element_type=F32)
        z = z + (cq - ck_ref[:, rows])
        if diagonal:
            qpos = lax.broadcasted_iota(jnp.int32, (t, t), 0)
            kpos = lax.broadcasted_iota(jnp.int32, (t, t), 1)
            z = jnp.where(kpos <= qpos, z, MASK_VALUE)
        z_ref[...] = z
        return jnp.max(z, axis=-1, keepdims=True)

    def accumulate(j, m, l, row_max):
        rows = pl.ds(pl.multiple_of(j * t, t), t)
        m_new = jnp.maximum(m, row_max)
        scale = jnp.exp(m - m_new)
        p = jnp.exp(z_ref[...] - m_new)
        l = scale * l + jnp.sum(p, axis=-1, keepdims=True)
        acc_ref[...] = scale * acc_ref[...] + jnp.dot(
            p.astype(BF16), v_ref[rows, :], preferred_element_type=F32)
        return m_new, l

    def step(n, carry):
        m, l, row_max = carry
        j = i - n
        m, l = accumulate(j, m, l, row_max)
        return m, l, logits(j - 1, False)

    acc_ref[...] = jnp.zeros_like(acc_ref)
    carry = (jnp.full((t, 1), MASK_VALUE, F32), jnp.zeros((t, 1), F32), logits(i, True))
    m, l, row_max = lax.fori_loop(0, i, step, carry)
    _, l = accumulate(0, m, l, row_max)
    o_ref[...] = (acc_ref[...] / l).astype(o_ref.dtype)


def _fox(qkv, cum_col, cum_row, n_heads, t_block):
    s = qkv.shape[0]
    d = ATT_HEAD_DIM
    t_block = min(t_block, s)
    return pl.pallas_call(
        _fox_body,
        out_shape=jax.ShapeDtypeStruct((s, n_heads * d), BF16),
        grid=(n_heads, s // t_block),
        in_specs=[pl.BlockSpec((t_block, d), lambda h, i: (i, h)),
                  pl.BlockSpec((s, d), lambda h, i: (0, n_heads + h)),
                  pl.BlockSpec((s, d), lambda h, i: (0, 2 * n_heads + h)),
                  pl.BlockSpec((None, t_block, 1), lambda h, i: (h, i, 0)),
                  pl.BlockSpec((None, 1, s), lambda h, i: (h, 0, 0))],
        out_specs=pl.BlockSpec((t_block, d), lambda h, i: (i, h)),
        scratch_shapes=[pltpu.VMEM((t_block, d), F32),
                        pltpu.VMEM((t_block, t_block), F32)],
        compiler_params=_params(2),
        name="fox_attn",
    )(qkv, qkv, qkv, cum_col, cum_row)


def _conv_outproj_ln_body(bg_ref, cg_ref, hd_ref, cgp_ref, hdp_ref, cw_ref, w_ref, h_ref,
                          g_ref, b_ref, of_ref, ob_ref):
    i = pl.program_id(0)
    u = cg_ref[...] * hd_ref[...]
    u_prev = jnp.where(i > 0, cgp_ref[...] * hdp_ref[...], 0.0)
    row8 = lax.broadcasted_iota(jnp.int32, u_prev.shape, 0)
    y = cw_ref[CONV_WIDTH - 1:CONV_WIDTH, :] * u
    for lag in range(1, CONV_WIDTH):
        shifted = pltpu.roll(u, lag, 0)
        head = jnp.where(row8 < lag, pltpu.roll(u_prev, lag, 0), shifted[:SUBLANES])
        shifted = jnp.concatenate([head, shifted[SUBLANES:]], axis=0)
        y += cw_ref[CONV_WIDTH - 1 - lag:CONV_WIDTH - lag, :] * shifted
    acc = jnp.dot((bg_ref[...] * y).astype(BF16), w_ref[...], preferred_element_type=F32)
    out = _layer_norm(ALPHA * h_ref[...] + acc, g_ref[...], b_ref[...])
    of_ref[...] = out
    ob_ref[...] = out.astype(BF16)


def _conv_outproj_ln(proj, conv_w, w, h, g, b, tm):
    m, d = h.shape
    tm = min(tm, m)
    blocks_per_tile = tm // SUBLANES
    col = lambda off: (lambda i: (i, off))
    prev = lambda off: (lambda i: (jnp.maximum(i * blocks_per_tile - 1, 0), off))
    row = lambda i: (i, 0)
    fixed = lambda i: (0, 0)
    return pl.pallas_call(
        _conv_outproj_ln_body,
        out_shape=(jax.ShapeDtypeStruct((m, d), F32),
                   jax.ShapeDtypeStruct((m, d), BF16)),
        grid=(m // tm,),
        in_specs=[pl.BlockSpec((tm, d), col(0)),
                  pl.BlockSpec((tm, d), col(1)),
                  pl.BlockSpec((tm, d), col(2)),
                  pl.BlockSpec((SUBLANES, d), prev(1)),
                  pl.BlockSpec((SUBLANES, d), prev(2)),
                  pl.BlockSpec((CONV_WIDTH, d), fixed),
                  pl.BlockSpec((d, d), fixed),
                  pl.BlockSpec((tm, d), row),
                  pl.BlockSpec((1, d), fixed),
                  pl.BlockSpec((1, d), fixed)],
        out_specs=(pl.BlockSpec((tm, d), row), pl.BlockSpec((tm, d), row)),
        compiler_params=_params(1),
        name="conv_outproj_ln",
    )(proj, proj, proj, proj, proj, conv_w, w, h, g.reshape(1, d), b.reshape(1, d))


def _tiles(s):
    return dict(
        proj_tm=min(1024, s), proj_tn=1024,
        out_tm=min(512, s), conv_tm=min(256, s),
        mlp_tm=min(1024, s), mlp_tf=512,
        hgrn_t=min(256, s), hgrn_heads=4,
        att_t=min(512, s),
        gate_t=min(256, s),
    )


def kernel(x, w_mix_a, norm_g_a, lb_logits, w_out_a, w_mix_b, w_out_b, w_mix_c, b_f_c,
           w_out_c, w_mix_d, conv_w_d, w_out_d, ln_mix_g, ln_mix_b, w_ff1, w_ff2,
           ln_ff_g, ln_ff_b):
    bsz, s, d = x.shape
    n_att_heads = d // ATT_HEAD_DIM
    tl = _tiles(s)
    outs = []
    for bi in range(bsz):
        h = x[bi]
        hb = h.astype(BF16)
        for i in range(DEPTH):
            m, j = i % N_MIXERS, i // N_MIXERS
            g, b = ln_mix_g[i], ln_mix_b[i]
            if m == 0:
                proj = _proj(hb, w_mix_a[j].astype(BF16), F32, tl["proj_tm"], tl["proj_tn"])
                y = _hgrn(proj, lb_logits, norm_g_a[j], i, tl["hgrn_t"], tl["hgrn_heads"])
                h, hb = _outproj_ln(y, w_out_a[j].astype(BF16), h, g, b, tl["out_tm"])
            elif m == 1:
                qkv = _proj(hb, w_mix_b[j].astype(BF16), BF16, tl["proj_tm"], tl["proj_tn"])
                y = _stickbreak(qkv, n_att_heads, tl["att_t"])
                h, hb = _outproj_ln(y, w_out_b[j].astype(BF16), h, g, b, tl["out_tm"])
            elif m == 2:
                wc = w_mix_c[j]
                qkv = _proj(hb, wc[:, :3 * d].astype(BF16), BF16, tl["proj_tm"], tl["proj_tn"])
                lane_pad = HGRN_HEAD_DIM - n_att_heads
                wf = jnp.pad(wc[:, 3 * d:], ((0, 0), (0, lane_pad))).astype(BF16)
                fg = _proj(hb, wf, F32, tl["proj_tm"], tl["proj_tn"])
                bf = jnp.pad(b_f_c[j], (0, lane_pad)).reshape(1, -1)
                cum = _fox_gates(fg, bf, tl["gate_t"])[:, :n_att_heads]
                cum_t = cum.T
                y = _fox(qkv, cum_t[:, :, None], cum_t[:, None, :], n_att_heads, tl["att_t"])
                h, hb = _outproj_ln(y, w_out_c[j].astype(BF16), h, g, b, tl["out_tm"])
            else:
                proj = _proj(hb, w_mix_d[j].astype(BF16), F32, tl["proj_tm"], tl["proj_tn"])
                h, hb = _conv_outproj_ln(proj, conv_w_d[j], w_out_d[j].astype(BF16), h, g, b,
                                         tl["conv_tm"])
            h, hb = _mlp_ln(hb, h, w_ff1[i].astype(BF16), w_ff2[i].astype(BF16),
                            ln_ff_g[i], ln_ff_b[i], tl["mlp_tm"], tl["mlp_tf"])
        outs.append(h)
    return jnp.stack(outs, axis=0)
```

```python
import functools

import jax
import jax.numpy as jnp
from jax import lax
from jax.experimental import pallas as pl
from jax.experimental.pallas import tpu as pltpu

DEPTH = 4
N_MIXERS = 4
HGRN_HEAD_DIM = 128
ATT_HEAD_DIM = 256
CHUNK = 16
CONV_WIDTH = 3
LN_EPS = 1e-5
RMS_EPS = 1e-6
ALPHA = (2.0 * DEPTH) ** 0.25
MASK_VALUE = -1e30
LOG2E = 1.4426950408889634
V7X_MXU_WIDTH = 256
SB_CUMSUM_WIDTH = V7X_MXU_WIDTH

V7X_VMEM_LIMIT_BYTES = 56 * 1024 * 1024
SUBLANES = 8

F32 = jnp.float32
BF16 = jnp.bfloat16


def _params(n_grid_axes, vmem_bytes=V7X_VMEM_LIMIT_BYTES):
    return pltpu.CompilerParams(
        dimension_semantics=("arbitrary",) * n_grid_axes,
        vmem_limit_bytes=vmem_bytes)


def _layer_norm(v, g, b):
    mu = jnp.mean(v, axis=-1, keepdims=True)
    c = v - mu
    var = jnp.mean(c * c, axis=-1, keepdims=True)
    return c * lax.rsqrt(var + LN_EPS) * g + b


def _split3(x):
    hi = x.astype(BF16)
    r1 = x - hi.astype(F32)
    mid = r1.astype(BF16)
    lo = (r1 - mid.astype(F32)).astype(BF16)
    return hi, mid, lo


def _dot3(mat01, x):
    hi, mid, lo = _split3(x)
    out = jnp.dot(mat01, lo, preferred_element_type=F32)
    out += jnp.dot(mat01, mid, preferred_element_type=F32)
    out += jnp.dot(mat01, hi, preferred_element_type=F32)
    return out


def _softplus(z):
    return jnp.maximum(z, 0.0) + jnp.log1p(jnp.exp(-jnp.abs(z)))


def _proj_body(x_ref, w_ref, o_ref):
    o_ref[...] = jnp.dot(x_ref[...], w_ref[...],
                         preferred_element_type=F32).astype(o_ref.dtype)


def _proj(x, w, out_dtype, tm, tn):
    m, k = x.shape
    n = w.shape[1]
    tm, tn = min(tm, m), min(tn, n)
    assert m % tm == 0 and n % tn == 0, (m, n, tm, tn)
    return pl.pallas_call(
        _proj_body,
        out_shape=jax.ShapeDtypeStruct((m, n), out_dtype),
        grid=(m // tm, n // tn),
        in_specs=[pl.BlockSpec((tm, k), lambda i, j: (i, 0)),
                  pl.BlockSpec((k, tn), lambda i, j: (0, j))],
        out_specs=pl.BlockSpec((tm, tn), lambda i, j: (i, j)),
        compiler_params=_params(2),
        name="proj",
    )(x, w)


def _outproj_ln_body(y_ref, w_ref, h_ref, g_ref, b_ref, of_ref, ob_ref):
    acc = jnp.dot(y_ref[...], w_ref[...], preferred_element_type=F32)
    out = _layer_norm(ALPHA * h_ref[...] + acc, g_ref[...], b_ref[...])
    of_ref[...] = out
    ob_ref[...] = out.astype(BF16)


def _outproj_ln(y, w, h, g, b, tm):
    m, d = h.shape
    tm = min(tm, m)
    row = lambda i: (i, 0)
    fixed = lambda i: (0, 0)
    return pl.pallas_call(
        _outproj_ln_body,
        out_shape=(jax.ShapeDtypeStruct((m, d), F32),
                   jax.ShapeDtypeStruct((m, d), BF16)),
        grid=(m // tm,),
        in_specs=[pl.BlockSpec((tm, d), row),
                  pl.BlockSpec((d, d), fixed),
                  pl.BlockSpec((tm, d), row),
                  pl.BlockSpec((1, d), fixed),
                  pl.BlockSpec((1, d), fixed)],
        out_specs=(pl.BlockSpec((tm, d), row), pl.BlockSpec((tm, d), row)),
        compiler_params=_params(1),
        name="outproj_ln",
    )(y, w, h, g.reshape(1, d), b.reshape(1, d))


def _mlp_ln_body(xb_ref, h_ref, w1_ref, w2_ref, g_ref, b_ref, of_ref, ob_ref):
    j = pl.program_id(1)

    @pl.when(j == 0)
    def _():
        of_ref[...] = ALPHA * h_ref[...]

    a = jnp.dot(xb_ref[...], w1_ref[...], preferred_element_type=F32)
    a = jnp.maximum(a, 0.0)
    of_ref[...] += jnp.dot((a * a).astype(BF16), w2_ref[...], preferred_element_type=F32)

    @pl.when(j == pl.num_programs(1) - 1)
    def _():
        out = _layer_norm(of_ref[...], g_ref[...], b_ref[...])
        of_ref[...] = out
        ob_ref[...] = out.astype(BF16)


def _mlp_ln(xb, h, w1, w2, g, b, tm, tf):
    m, d = h.shape
    f = w1.shape[1]
    tm, tf = min(tm, m), min(tf, f)
    row = lambda i, j: (i, 0)
    fixed = lambda i, j: (0, 0)
    return pl.pallas_call(
        _mlp_ln_body,
        out_shape=(jax.ShapeDtypeStruct((m, d), F32),
                   jax.ShapeDtypeStruct((m, d), BF16)),
        grid=(m // tm, f // tf),
        in_specs=[pl.BlockSpec((tm, d), row),
                  pl.BlockSpec((tm, d), row, pipeline_mode=pl.Buffered(1)),
                  pl.BlockSpec((d, tf), lambda i, j: (0, j)),
                  pl.BlockSpec((tf, d), lambda i, j: (j, 0)),
                  pl.BlockSpec((1, d), fixed),
                  pl.BlockSpec((1, d), fixed)],
        out_specs=(pl.BlockSpec((tm, d), row),
                   pl.BlockSpec((tm, d), row, pipeline_mode=pl.Buffered(1))),
        compiler_params=_params(2),
        name="mlp_ln",
    )(xb, h, w1, w2, g.reshape(1, d), b.reshape(1, d))


def _hgrn_body(layer, n_heads, q_ref, f_ref, i_ref, g_ref, lbl_ref, ng_ref, o_ref,
               st_ref, qd_ref, kd_ref, v_ref, dl_ref, acc_ref):
    t = q_ref.shape[0]
    dk = HGRN_HEAD_DIM

    @pl.when(pl.program_id(1) == 0)
    def _():
        st_ref[...] = jnp.zeros_like(st_ref)

    lg = lbl_ref[...]
    e = jnp.exp(lg - jnp.max(lg, axis=0, keepdims=True))
    lb = jnp.sum(e[:layer + 1], axis=0, keepdims=True) / jnp.sum(e, axis=0, keepdims=True)

    f = f_ref[...]
    e = jnp.exp(-jnp.abs(f))
    log_sig = jnp.minimum(f, 0.0) - jnp.log(1.0 + e)
    a = jnp.log(lb)
    b = jnp.log1p(-lb) + log_sig
    log_f = jnp.maximum(a, b) + jnp.log(1.0 + jnp.exp(-jnp.abs(a - b)))
    k = (1.0 - lb) * (jnp.where(f >= 0.0, e, 1.0) / (1.0 + e))
    q = q_ref[...]
    q = q * jax.nn.sigmoid(q) * dk ** -0.5

    r = lax.broadcasted_iota(jnp.int32, (t, t), 0)
    c = lax.broadcasted_iota(jnp.int32, (t, t), 1)
    same_chunk = (r // CHUNK) == (c // CHUNK)
    causal = same_chunk & (c <= r)
    tri = jnp.where(causal, 1.0, 0.0).astype(BF16)
    ones = jnp.where(same_chunk, 1.0, 0.0).astype(BF16)
    bc = _dot3(tri, log_f)
    b_last = _dot3(ones, log_f)

    qd_ref[...] = (q * jnp.exp(bc)).astype(BF16)
    k_inv = (k * jnp.exp(-bc)).astype(BF16)
    kd_ref[...] = k * jnp.exp(b_last - bc)
    dl_ref[...] = jnp.exp(b_last)
    v_ref[...] = i_ref[...].astype(BF16)

    for h in range(n_heads):
        cols = slice(h * dk, (h + 1) * dk)
        sc = lax.dot_general(qd_ref[:, cols], k_inv[:, cols], (((1,), (1,)), ((), ())),
                             preferred_element_type=F32)
        sc = jnp.where(causal, sc, 0.0).astype(BF16)
        acc_ref[:, cols] = jnp.dot(sc, v_ref[:, cols], preferred_element_type=F32)

    n_chunks = t // CHUNK
    chunk_of_row = lax.broadcasted_iota(jnp.int32, (t, dk), 0) // CHUNK
    for h in range(n_heads):
        cols = slice(h * dk, (h + 1) * dk)
        kd = kd_ref[:, cols]
        kd_by_chunk = jnp.concatenate(
            [jnp.where(chunk_of_row == ci, kd, 0.0) for ci in range(n_chunks)], axis=1)
        upd = lax.dot_general(v_ref[:, cols], kd_by_chunk.astype(BF16),
                              (((0,), (0,)), ((), ())), preferred_element_type=F32)
        st = st_ref[h]
        for ci in range(n_chunks):
            rows = slice(ci * CHUNK, (ci + 1) * CHUNK)
            inter = lax.dot_general(qd_ref[rows, cols], st.astype(BF16),
                                    (((1,), (1,)), ((), ())), preferred_element_type=F32)
            acc_ref[rows, cols] += inter
            st = dl_ref[ci * CHUNK:ci * CHUNK + 1, cols] * st + upd[:, ci * dk:(ci + 1) * dk]
        st_ref[h] = st

    ng = ng_ref[...]
    g = g_ref[...]
    gate = g * jax.nn.sigmoid(g)
    for h in range(n_heads):
        cols = slice(h * dk, (h + 1) * dk)
        o = acc_ref[:, cols]
        o = o * lax.rsqrt(jnp.mean(o * o, axis=-1, keepdims=True) + RMS_EPS) * ng
        o_ref[:, cols] = (o * gate[:, cols]).astype(BF16)


def _hgrn(proj, lb_logits, norm_g, layer, t_block, heads_per_step):
    s, d4 = proj.shape
    d = d4 // 4
    t_block = min(t_block, s)
    gw = heads_per_step * HGRN_HEAD_DIM
    n_groups = d // gw
    n_lb = lb_logits.shape[0]
    col = lambda off: (lambda hg, tb: (tb, off * n_groups + hg))
    return pl.pallas_call(
        functools.partial(_hgrn_body, layer, heads_per_step),
        out_shape=jax.ShapeDtypeStruct((s, d), BF16),
        grid=(n_groups, s // t_block),
        in_specs=[pl.BlockSpec((t_block, gw), col(0)),
                  pl.BlockSpec((t_block, gw), col(1)),
                  pl.BlockSpec((t_block, gw), col(2)),
                  pl.BlockSpec((t_block, gw), col(3)),
                  pl.BlockSpec((n_lb, gw), lambda hg, tb: (0, hg)),
                  pl.BlockSpec((1, HGRN_HEAD_DIM), lambda hg, tb: (0, 0))],
        out_specs=pl.BlockSpec((t_block, gw), lambda hg, tb: (tb, hg)),
        scratch_shapes=[pltpu.VMEM((heads_per_step, HGRN_HEAD_DIM, HGRN_HEAD_DIM), F32),
                        pltpu.VMEM((t_block, gw), BF16),
                        pltpu.VMEM((t_block, gw), F32),
                        pltpu.VMEM((t_block, gw), BF16),
                        pltpu.VMEM((t_block, gw), F32),
                        pltpu.VMEM((t_block, gw), F32)],
        compiler_params=_params(2),
        name="hgrn2",
    )(proj, proj, proj, proj, lb_logits, norm_g.reshape(1, HGRN_HEAD_DIM))


def _stickbreak_body(q_ref, k_ref, v_ref, o_ref, acc_ref, lw_ref):
    tq, d = q_ref.shape
    tk = lw_ref.shape[1]
    cw = min(SB_CUMSUM_WIDTH, tk)
    first_row = pl.program_id(1) * tq
    diag = first_row // tk
    q = q_ref[...] * d ** -0.5
    r = lax.broadcasted_iota(jnp.int32, (cw, cw), 0)
    c = lax.broadcasted_iota(jnp.int32, (cw, cw), 1)
    incl_rev = jnp.where(r >= c, 1.0, 0.0).astype(BF16)

    def log_weights(j, diagonal):
        rows = pl.ds(pl.multiple_of(j * tk, tk), tk)
        z = lax.dot_general(q, k_ref[rows, :], (((1,), (1,)), ((), ())),
                            preferred_element_type=F32)
        if diagonal:
            ahead = (lax.broadcasted_iota(jnp.int32, (tq, tk), 1)
                     - lax.broadcasted_iota(jnp.int32, (tq, tk), 0))
            z = jnp.where(ahead < first_row - j * tk, z, MASK_VALUE)
        sp = jnp.maximum(z, 0.0) + jnp.log(1.0 + jnp.exp2(jnp.abs(z) * -LOG2E))
        sp = sp.astype(BF16)
        later = None
        for sub in reversed(range(tk // cw)):
            cols = slice(sub * cw, (sub + 1) * cw)
            within = jnp.dot(sp[:, cols], incl_rev, preferred_element_type=F32)
            total = within if later is None else within + later
            lw_ref[:, cols] = z[:, cols] - total
            later = total[:, 0:1]
        return later

    def accumulate(j, offs):
        rows = pl.ds(pl.multiple_of(j * tk, tk), tk)
        w = jnp.exp(lw_ref[...] - offs)
        acc_ref[...] += jnp.dot(w.astype(BF16), v_ref[rows, :], preferred_element_type=F32)

    def step(n, carry):
        offs, block_sum = carry
        j = diag - n
        accumulate(j, offs)
        return offs + block_sum, log_weights(j - 1, False)

    acc_ref[...] = jnp.zeros_like(acc_ref)
    carry = (jnp.zeros((tq, 1), F32), log_weights(diag, True))
    offs, _ = lax.fori_loop(0, diag, step, carry)
    accumulate(0, offs)
    o_ref[...] = acc_ref[...].astype(o_ref.dtype)


def _stickbreak(qkv, n_heads, tq, tk):
    s = qkv.shape[0]
    d = ATT_HEAD_DIM
    tq, tk = min(tq, s), min(tk, s)
    assert s % tk == 0 and tk % tq == 0, (s, tq, tk)
    return pl.pallas_call(
        _stickbreak_body,
        out_shape=jax.ShapeDtypeStruct((s, n_heads * d), BF16),
        grid=(n_heads, s // tq),
        in_specs=[pl.BlockSpec((tq, d), lambda h, i: (i, h)),
                  pl.BlockSpec((s, d), lambda h, i: (0, n_heads + h)),
                  pl.BlockSpec((s, d), lambda h, i: (0, 2 * n_heads + h))],
        out_specs=pl.BlockSpec((tq, d), lambda h, i: (i, h)),
        scratch_shapes=[pltpu.VMEM((tq, d), F32),
                        pltpu.VMEM((tq, tk), F32)],
        compiler_params=_params(2),
        name="stickbreak_attn",
    )(qkv, qkv, qkv)


def _fox_gate_body(fg_ref, bf_ref, cum_ref, carry_ref):
    t = fg_ref.shape[0]

    @pl.when(pl.program_id(0) == 0)
    def _():
        carry_ref[...] = jnp.zeros_like(carry_ref)

    x = fg_ref[...] + bf_ref[...]
    log_f = jnp.minimum(x, 0.0) - jnp.log1p(jnp.exp(-jnp.abs(x)))
    r = lax.broadcasted_iota(jnp.int32, (t, t), 0)
    c = lax.broadcasted_iota(jnp.int32, (t, t), 1)
    tri = jnp.where(c <= r, 1.0, 0.0).astype(BF16)
    cum = _dot3(tri, log_f) + carry_ref[...]
    cum_ref[...] = cum
    carry_ref[...] = cum[t - 1:t, :]


def _fox_gates(fg, bf, t_block):
    s, w = fg.shape
    t_block = min(t_block, s)
    return pl.pallas_call(
        _fox_gate_body,
        out_shape=jax.ShapeDtypeStruct((s, w), F32),
        grid=(s // t_block,),
        in_specs=[pl.BlockSpec((t_block, w), lambda i: (i, 0)),
                  pl.BlockSpec((1, w), lambda i: (0, 0))],
        out_specs=pl.BlockSpec((t_block, w), lambda i: (i, 0)),
        scratch_shapes=[pltpu.VMEM((1, w), F32)],
        compiler_params=_params(1),
        name="fox_gates",
    )(fg, bf)


def _fox_body(q_ref, k_ref, v_ref, cq_ref, ck_ref, o_ref, acc_ref, z_ref):
    tq, d = q_ref.shape
    tk = z_ref.shape[1]
    first_row = pl.program_id(1) * tq
    diag = first_row // tk
    q = q_ref[...] * d ** -0.5
    cq = cq_ref[...]

    def logits(j, diagonal):
        rows = pl.ds(pl.multiple_of(j * tk, tk), tk)
        z = lax.dot_general(q, k_ref[rows, :], (((1,), (1,)), ((), ())),
                            preferred_element_type=F32)
        z = (z + cq) - ck_ref[:, rows]
        if diagonal:
            ahead = (lax.broadcasted_iota(jnp.int32, (tq, tk), 1)
                     - lax.broadcasted_iota(jnp.int32, (tq, tk), 0))
            z = jnp.where(ahead <= first_row - j * tk, z, MASK_VALUE)
        z_ref[...] = z
        return jnp.max(z, axis=-1, keepdims=True)

    def accumulate(j, m, l, row_max):
        rows = pl.ds(pl.multiple_of(j * tk, tk), tk)
        m_new = jnp.maximum(m, row_max)
        scale = jnp.exp(m - m_new)
        p = jnp.exp(z_ref[...] - m_new)
        l = scale * l + jnp.sum(p, axis=-1, keepdims=True)
        acc_ref[...] = scale * acc_ref[...] + jnp.dot(
            p.astype(BF16), v_ref[rows, :], preferred_element_type=F32)
        return m_new, l

    def step(n, carry):
        m, l, row_max = carry
        j = diag - n
        m, l = accumulate(j, m, l, row_max)
        return m, l, logits(j - 1, False)

    acc_ref[...] = jnp.zeros_like(acc_ref)
    carry = (jnp.full((tq, 1), MASK_VALUE, F32), jnp.zeros((tq, 1), F32), logits(diag, True))
    m, l, row_max = lax.fori_loop(0, diag, step, carry)
    _, l = accumulate(0, m, l, row_max)
    o_ref[...] = (acc_ref[...] / l).astype(o_ref.dtype)


def _fox(qkv, cum_col, cum_row, n_heads, tq, tk):
    s = qkv.shape[0]
    d = ATT_HEAD_DIM
    tq, tk = min(tq, s), min(tk, s)
    assert s % tk == 0 and tk % tq == 0, (s, tq, tk)
    return pl.pallas_call(
        _fox_body,
        out_shape=jax.ShapeDtypeStruct((s, n_heads * d), BF16),
        grid=(n_heads, s // tq),
        in_specs=[pl.BlockSpec((tq, d), lambda h, i: (i, h)),
                  pl.BlockSpec((s, d), lambda h, i: (0, n_heads + h)),
                  pl.BlockSpec((s, d), lambda h, i: (0, 2 * n_heads + h)),
                  pl.BlockSpec((None, tq, 1), lambda h, i: (h, i, 0)),
                  pl.BlockSpec((None, 1, s), lambda h, i: (h, 0, 0))],
        out_specs=pl.BlockSpec((tq, d), lambda h, i: (i, h)),
        scratch_shapes=[pltpu.VMEM((tq, d), F32),
                        pltpu.VMEM((tq, tk), F32)],
        compiler_params=_params(2),
        name="fox_attn",
    )(qkv, qkv, qkv, cum_col, cum_row)


def _conv_outproj_ln_body(bg_ref, cg_ref, hd_ref, cgp_ref, hdp_ref, cw_ref, w_ref, h_ref,
                          g_ref, b_ref, of_ref, ob_ref):
    i = pl.program_id(0)
    u = cg_ref[...] * hd_ref[...]
    u_prev = jnp.where(i > 0, cgp_ref[...] * hdp_ref[...], 0.0)
    row8 = lax.broadcasted_iota(jnp.int32, u_prev.shape, 0)
    y = cw_ref[CONV_WIDTH - 1:CONV_WIDTH, :] * u
    for lag in range(1, CONV_WIDTH):
        shifted = pltpu.roll(u, lag, 0)
        head = jnp.where(row8 < lag, pltpu.roll(u_prev, lag, 0), shifted[:SUBLANES])
        shifted = jnp.concatenate([head, shifted[SUBLANES:]], axis=0)
        y += cw_ref[CONV_WIDTH - 1 - lag:CONV_WIDTH - lag, :] * shifted
    acc = jnp.dot((bg_ref[...] * y).astype(BF16), w_ref[...], preferred_element_type=F32)
    out = _layer_norm(ALPHA * h_ref[...] + acc, g_ref[...], b_ref[...])
    of_ref[...] = out
    ob_ref[...] = out.astype(BF16)


def _conv_outproj_ln(proj, conv_w, w, h, g, b, tm):
    m, d = h.shape
    tm = min(tm, m)
    blocks_per_tile = tm // SUBLANES
    col = lambda off: (lambda i: (i, off))
    prev = lambda off: (lambda i: (jnp.maximum(i * blocks_per_tile - 1, 0), off))
    row = lambda i: (i, 0)
    fixed = lambda i: (0, 0)
    return pl.pallas_call(
        _conv_outproj_ln_body,
        out_shape=(jax.ShapeDtypeStruct((m, d), F32),
                   jax.ShapeDtypeStruct((m, d), BF16)),
        grid=(m // tm,),
        in_specs=[pl.BlockSpec((tm, d), col(0)),
                  pl.BlockSpec((tm, d), col(1)),
                  pl.BlockSpec((tm, d), col(2)),
                  pl.BlockSpec((SUBLANES, d), prev(1)),
                  pl.BlockSpec((SUBLANES, d), prev(2)),
                  pl.BlockSpec((CONV_WIDTH, d), fixed),
                  pl.BlockSpec((d, d), fixed),
                  pl.BlockSpec((tm, d), row),
                  pl.BlockSpec((1, d), fixed),
                  pl.BlockSpec((1, d), fixed)],
        out_specs=(pl.BlockSpec((tm, d), row), pl.BlockSpec((tm, d), row)),
        compiler_params=_params(1),
        name="conv_outproj_ln",
    )(proj, proj, proj, proj, proj, conv_w, w, h, g.reshape(1, d), b.reshape(1, d))


def _tiles(s):
    return dict(
        proj_tm=min(1024, s), proj_tn=1024,
        out_tm=min(512, s), conv_tm=min(256, s),
        mlp_tm=min(1024, s), mlp_tf=512,
        hgrn_t=min(256, s), hgrn_heads=4,
        sb_tq=min(1024, s), sb_tk=min(1024, s), fox_tq=min(1024, s), fox_tk=min(1024, s),
        gate_t=min(256, s),
    )


def kernel(x, w_mix_a, norm_g_a, lb_logits, w_out_a, w_mix_b, w_out_b, w_mix_c, b_f_c,
           w_out_c, w_mix_d, conv_w_d, w_out_d, ln_mix_g, ln_mix_b, w_ff1, w_ff2,
           ln_ff_g, ln_ff_b):
    bsz, s, d = x.shape
    n_att_heads = d // ATT_HEAD_DIM
    tl = _tiles(s)
    outs = []
    for bi in range(bsz):
        h = x[bi]
        hb = h.astype(BF16)
        for i in range(DEPTH):
            m, j = i % N_MIXERS, i // N_MIXERS
            g, b = ln_mix_g[i], ln_mix_b[i]
            if m == 0:
                proj = _proj(hb, w_mix_a[j].astype(BF16), F32, tl["proj_tm"], tl["proj_tn"])
                y = _hgrn(proj, lb_logits, norm_g_a[j], i, tl["hgrn_t"], tl["hgrn_heads"])
                h, hb = _outproj_ln(y, w_out_a[j].astype(BF16), h, g, b, tl["out_tm"])
            elif m == 1:
                qkv = _proj(hb, w_mix_b[j].astype(BF16), BF16, tl["proj_tm"], tl["proj_tn"])
                y = _stickbreak(qkv, n_att_heads, tl["sb_tq"], tl["sb_tk"])
                h, hb = _outproj_ln(y, w_out_b[j].astype(BF16), h, g, b, tl["out_tm"])
            elif m == 2:
                wc = w_mix_c[j]
                qkv = _proj(hb, wc[:, :3 * d].astype(BF16), BF16, tl["proj_tm"], tl["proj_tn"])
                lane_pad = HGRN_HEAD_DIM - n_att_heads
                wf = jnp.pad(wc[:, 3 * d:], ((0, 0), (0, lane_pad))).astype(BF16)
                fg = _proj(hb, wf, F32, tl["proj_tm"], tl["proj_tn"])
                bf = jnp.pad(b_f_c[j], (0, lane_pad)).reshape(1, -1)
                cum = _fox_gates(fg, bf, tl["gate_t"])[:, :n_att_heads]
                cum_t = cum.T
                y = _fox(qkv, cum_t[:, :, None], cum_t[:, None, :], n_att_heads,
                         tl["fox_tq"], tl["fox_tk"])
                h, hb = _outproj_ln(y, w_out_c[j].astype(BF16), h, g, b, tl["out_tm"])
            else:
                proj = _proj(hb, w_mix_d[j].astype(BF16), F32, tl["proj_tm"], tl["proj_tn"])
                h, hb = _conv_outproj_ln(proj, conv_w_d[j], w_out_d[j].astype(BF16), h, g, b,
                                         tl["conv_tm"])
            h, hb = _mlp_ln(hb, h, w_ff1[i].astype(BF16), w_ff2[i].astype(BF16),
                            ln_ff_g[i], ln_ff_b[i], tl["mlp_tm"], tl["mlp_tf"])
        outs.append(h)
    return outs[0][None] if bsz == 1 else jnp.stack(outs, axis=0)
```

```python
import functools

import jax
import jax.numpy as jnp
from jax import lax
from jax.experimental import pallas as pl
from jax.experimental.pallas import tpu as pltpu

DEPTH = 4
N_MIXERS = 4
HGRN_HEAD_DIM = 128
ATT_HEAD_DIM = 256
CHUNK = 16
CONV_WIDTH = 3
LN_EPS = 1e-5
RMS_EPS = 1e-6
ALPHA = (2.0 * DEPTH) ** 0.25
MASK_VALUE = -1e30
UNDERFLOW_MARGIN = 100.0
LOG2E = 1.4426950408889634
V7X_MXU_WIDTH = 256
SB_CUMSUM_WIDTH = V7X_MXU_WIDTH

V7X_VMEM_LIMIT_BYTES = 56 * 1024 * 1024
SUBLANES = 8
LANES = 128

F32 = jnp.float32
BF16 = jnp.bfloat16


def _params(n_grid_axes, vmem_bytes=V7X_VMEM_LIMIT_BYTES):
    return pltpu.CompilerParams(
        dimension_semantics=("arbitrary",) * n_grid_axes,
        vmem_limit_bytes=vmem_bytes)


def _layer_norm(v, g, b):
    mu = jnp.mean(v, axis=-1, keepdims=True)
    c = v - mu
    var = jnp.mean(c * c, axis=-1, keepdims=True)
    return c * lax.rsqrt(var + LN_EPS) * g + b


def _split3(x):
    hi = x.astype(BF16)
    r1 = x - hi.astype(F32)
    mid = r1.astype(BF16)
    lo = (r1 - mid.astype(F32)).astype(BF16)
    return hi, mid, lo


def _dot3(mat01, x):
    hi, mid, lo = _split3(x)
    out = jnp.dot(mat01, lo, preferred_element_type=F32)
    out += jnp.dot(mat01, mid, preferred_element_type=F32)
    out += jnp.dot(mat01, hi, preferred_element_type=F32)
    return out


def _softplus(z):
    return jnp.maximum(z, 0.0) + jnp.log1p(jnp.exp(-jnp.abs(z)))


def _proj_body(x_ref, w_ref, o_ref):
    o_ref[...] = jnp.dot(x_ref[...], w_ref[...],
                         preferred_element_type=F32).astype(o_ref.dtype)


def _proj(x, w, out_dtype, tm, tn):
    m, k = x.shape
    n = w.shape[1]
    tm, tn = min(tm, m), min(tn, n)
    assert m % tm == 0 and n % tn == 0, (m, n, tm, tn)
    return pl.pallas_call(
        _proj_body,
        out_shape=jax.ShapeDtypeStruct((m, n), out_dtype),
        grid=(m // tm, n // tn),
        in_specs=[pl.BlockSpec((tm, k), lambda i, j: (i, 0)),
                  pl.BlockSpec((k, tn), lambda i, j: (0, j))],
        out_specs=pl.BlockSpec((tm, tn), lambda i, j: (i, j)),
        compiler_params=_params(2),
        name="proj",
    )(x, w)


def _outproj_ln_body(y_ref, w_ref, h_ref, g_ref, b_ref, of_ref, ob_ref):
    acc = jnp.dot(y_ref[...], w_ref[...], preferred_element_type=F32)
    out = _layer_norm(ALPHA * h_ref[...] + acc, g_ref[...], b_ref[...])
    of_ref[...] = out
    ob_ref[...] = out.astype(BF16)


def _outproj_ln(y, w, h, g, b, tm):
    m, d = h.shape
    tm = min(tm, m)
    row = lambda i: (i, 0)
    fixed = lambda i: (0, 0)
    return pl.pallas_call(
        _outproj_ln_body,
        out_shape=(jax.ShapeDtypeStruct((m, d), F32),
                   jax.ShapeDtypeStruct((m, d), BF16)),
        grid=(m // tm,),
        in_specs=[pl.BlockSpec((tm, d), row),
                  pl.BlockSpec((d, d), fixed),
                  pl.BlockSpec((tm, d), row),
                  pl.BlockSpec((1, d), fixed),
                  pl.BlockSpec((1, d), fixed)],
        out_specs=(pl.BlockSpec((tm, d), row), pl.BlockSpec((tm, d), row)),
        compiler_params=_params(1),
        name="outproj_ln",
    )(y, w, h, g.reshape(1, d), b.reshape(1, d))


def _mlp_ln_body(xb_ref, h_ref, w1_ref, w2_ref, g_ref, b_ref, of_ref, ob_ref):
    j = pl.program_id(1)

    @pl.when(j == 0)
    def _():
        of_ref[...] = ALPHA * h_ref[...]

    a = jnp.dot(xb_ref[...], w1_ref[...], preferred_element_type=F32)
    a = jnp.maximum(a, 0.0)
    of_ref[...] += jnp.dot((a * a).astype(BF16), w2_ref[...], preferred_element_type=F32)

    @pl.when(j == pl.num_programs(1) - 1)
    def _():
        out = _layer_norm(of_ref[...], g_ref[...], b_ref[...])
        of_ref[...] = out
        ob_ref[...] = out.astype(BF16)


def _mlp_ln(xb, h, w1, w2, g, b, tm, tf):
    m, d = h.shape
    f = w1.shape[1]
    tm, tf = min(tm, m), min(tf, f)
    row = lambda i, j: (i, 0)
    fixed = lambda i, j: (0, 0)
    return pl.pallas_call(
        _mlp_ln_body,
        out_shape=(jax.ShapeDtypeStruct((m, d), F32),
                   jax.ShapeDtypeStruct((m, d), BF16)),
        grid=(m // tm, f // tf),
        in_specs=[pl.BlockSpec((tm, d), row),
                  pl.BlockSpec((tm, d), row, pipeline_mode=pl.Buffered(1)),
                  pl.BlockSpec((d, tf), lambda i, j: (0, j)),
                  pl.BlockSpec((tf, d), lambda i, j: (j, 0)),
                  pl.BlockSpec((1, d), fixed),
                  pl.BlockSpec((1, d), fixed)],
        out_specs=(pl.BlockSpec((tm, d), row),
                   pl.BlockSpec((tm, d), row, pipeline_mode=pl.Buffered(1))),
        compiler_params=_params(2),
        name="mlp_ln",
    )(xb, h, w1, w2, g.reshape(1, d), b.reshape(1, d))


def _hgrn_body(layer, n_heads, q_ref, f_ref, i_ref, g_ref, lbl_ref, ng_ref, o_ref,
               st_ref, qd_ref, kd_ref, v_ref, dl_ref, acc_ref):
    t = q_ref.shape[0]
    dk = HGRN_HEAD_DIM

    @pl.when(pl.program_id(1) == 0)
    def _():
        st_ref[...] = jnp.zeros_like(st_ref)

    lg = lbl_ref[...]
    e = jnp.exp(lg - jnp.max(lg, axis=0, keepdims=True))
    lb = jnp.sum(e[:layer + 1], axis=0, keepdims=True) / jnp.sum(e, axis=0, keepdims=True)

    f = f_ref[...]
    e = jnp.exp(-jnp.abs(f))
    log_sig = jnp.minimum(f, 0.0) - jnp.log(1.0 + e)
    a = jnp.log(lb)
    b = jnp.log1p(-lb) + log_sig
    log_f = jnp.maximum(a, b) + jnp.log(1.0 + jnp.exp(-jnp.abs(a - b)))
    k = (1.0 - lb) * (jnp.where(f >= 0.0, e, 1.0) / (1.0 + e))
    q = q_ref[...]
    q = q * jax.nn.sigmoid(q) * dk ** -0.5

    r = lax.broadcasted_iota(jnp.int32, (t, t), 0)
    c = lax.broadcasted_iota(jnp.int32, (t, t), 1)
    same_chunk = (r // CHUNK) == (c // CHUNK)
    causal = same_chunk & (c <= r)
    tri = jnp.where(causal, 1.0, 0.0).astype(BF16)
    ones = jnp.where(same_chunk, 1.0, 0.0).astype(BF16)
    bc = _dot3(tri, log_f)
    b_last = _dot3(ones, log_f)

    qd_ref[...] = (q * jnp.exp(bc)).astype(BF16)
    k_inv = (k * jnp.exp(-bc)).astype(BF16)
    kd_ref[...] = k * jnp.exp(b_last - bc)
    dl_ref[...] = jnp.exp(b_last)
    v_ref[...] = i_ref[...].astype(BF16)

    for h in range(n_heads):
        cols = slice(h * dk, (h + 1) * dk)
        sc = lax.dot_general(qd_ref[:, cols], k_inv[:, cols], (((1,), (1,)), ((), ())),
                             preferred_element_type=F32)
        sc = jnp.where(causal, sc, 0.0).astype(BF16)
        acc_ref[:, cols] = jnp.dot(sc, v_ref[:, cols], preferred_element_type=F32)

    n_chunks = t // CHUNK
    chunk_of_row = lax.broadcasted_iota(jnp.int32, (t, dk), 0) // CHUNK
    for h in range(n_heads):
        cols = slice(h * dk, (h + 1) * dk)
        kd = kd_ref[:, cols]
        kd_by_chunk = jnp.concatenate(
            [jnp.where(chunk_of_row == ci, kd, 0.0) for ci in range(n_chunks)], axis=1)
        upd = lax.dot_general(v_ref[:, cols], kd_by_chunk.astype(BF16),
                              (((0,), (0,)), ((), ())), preferred_element_type=F32)
        st = st_ref[h]
        for ci in range(n_chunks):
            rows = slice(ci * CHUNK, (ci + 1) * CHUNK)
            inter = lax.dot_general(qd_ref[rows, cols], st.astype(BF16),
                                    (((1,), (1,)), ((), ())), preferred_element_type=F32)
            acc_ref[rows, cols] += inter
            st = dl_ref[ci * CHUNK:ci * CHUNK + 1, cols] * st + upd[:, ci * dk:(ci + 1) * dk]
        st_ref[h] = st

    ng = ng_ref[...]
    g = g_ref[...]
    gate = g * jax.nn.sigmoid(g)
    for h in range(n_heads):
        cols = slice(h * dk, (h + 1) * dk)
        o = acc_ref[:, cols]
        o = o * lax.rsqrt(jnp.mean(o * o, axis=-1, keepdims=True) + RMS_EPS) * ng
        o_ref[:, cols] = (o * gate[:, cols]).astype(BF16)


def _hgrn(proj, lb_logits, norm_g, layer, t_block, heads_per_step):
    s, d4 = proj.shape
    d = d4 // 4
    t_block = min(t_block, s)
    gw = heads_per_step * HGRN_HEAD_DIM
    n_groups = d // gw
    n_lb = lb_logits.shape[0]
    col = lambda off: (lambda hg, tb: (tb, off * n_groups + hg))
    return pl.pallas_call(
        functools.partial(_hgrn_body, layer, heads_per_step),
        out_shape=jax.ShapeDtypeStruct((s, d), BF16),
        grid=(n_groups, s // t_block),
        in_specs=[pl.BlockSpec((t_block, gw), col(0)),
                  pl.BlockSpec((t_block, gw), col(1)),
                  pl.BlockSpec((t_block, gw), col(2)),
                  pl.BlockSpec((t_block, gw), col(3)),
                  pl.BlockSpec((n_lb, gw), lambda hg, tb: (0, hg)),
                  pl.BlockSpec((1, HGRN_HEAD_DIM), lambda hg, tb: (0, 0))],
        out_specs=pl.BlockSpec((t_block, gw), lambda hg, tb: (tb, hg)),
        scratch_shapes=[pltpu.VMEM((heads_per_step, HGRN_HEAD_DIM, HGRN_HEAD_DIM), F32),
                        pltpu.VMEM((t_block, gw), BF16),
                        pltpu.VMEM((t_block, gw), F32),
                        pltpu.VMEM((t_block, gw), BF16),
                        pltpu.VMEM((t_block, gw), F32),
                        pltpu.VMEM((t_block, gw), F32)],
        compiler_params=_params(2),
        name="hgrn2",
    )(proj, proj, proj, proj, lb_logits, norm_g.reshape(1, HGRN_HEAD_DIM))


def _max_sq_row_norm(ref, chunk):
    def body(c, best):
        x = ref[pl.ds(pl.multiple_of(c * chunk, chunk), chunk), :].astype(F32)
        sq = jnp.sum(x * x, axis=-1, keepdims=True)
        return jnp.maximum(best, jnp.max(sq, axis=0, keepdims=True))
    return lax.fori_loop(0, ref.shape[0] // chunk, body, jnp.zeros((1, 1), F32))


def _stickbreak_body(q_ref, k_ref, v_ref, o_ref, acc_ref, lw_ref, knorm_ref):
    tq, d = q_ref.shape
    tk = lw_ref.shape[1]
    cw = min(SB_CUMSUM_WIDTH, tk)
    first_row = pl.program_id(1) * tq
    diag = first_row // tk
    q = q_ref[...] * d ** -0.5
    r = lax.broadcasted_iota(jnp.int32, (cw, cw), 0)
    c = lax.broadcasted_iota(jnp.int32, (cw, cw), 1)
    incl_rev = jnp.where(r >= c, 1.0, 0.0).astype(BF16)

    def log_weights(j, diagonal):
        rows = pl.ds(pl.multiple_of(j * tk, tk), tk)
        z = lax.dot_general(q, k_ref[rows, :], (((1,), (1,)), ((), ())),
                            preferred_element_type=F32)
        if diagonal:
            ahead = (lax.broadcasted_iota(jnp.int32, (tq, tk), 1)
                     - lax.broadcasted_iota(jnp.int32, (tq, tk), 0))
            z = jnp.where(ahead < first_row - j * tk, z, MASK_VALUE)
        sp = jnp.maximum(z, 0.0) + jnp.log(1.0 + jnp.exp2(jnp.abs(z) * -LOG2E))
        sp = sp.astype(BF16)
        later = None
        for sub in reversed(range(tk // cw)):
            cols = slice(sub * cw, (sub + 1) * cw)
            within = jnp.dot(sp[:, cols], incl_rev, preferred_element_type=F32)
            total = within if later is None else within + later
            lw_ref[:, cols] = z[:, cols] - total
            later = total[:, 0:1]
        return later

    def accumulate(j, offs):
        rows = pl.ds(pl.multiple_of(j * tk, tk), tk)
        w = jnp.exp(lw_ref[...] - offs)
        acc_ref[...] += jnp.dot(w.astype(BF16), v_ref[rows, :], preferred_element_type=F32)

    @pl.when(pl.program_id(1) == 0)
    def _():
        knorm_ref[...] = jnp.broadcast_to(_max_sq_row_norm(k_ref, tk), knorm_ref.shape)

    qf = q.astype(F32)
    q_sq = jnp.max(jnp.sum(qf * qf, axis=-1, keepdims=True), axis=0, keepdims=True)
    slack = jnp.sqrt(q_sq * knorm_ref[0:1, 0:1]) * 2.0 ** -8

    def not_done(carry):
        n, _, _, done = carry
        return jnp.logical_and(n < diag, done == 0)

    def step(carry):
        n, offs, block_sum, _ = carry
        j = diag - n
        accumulate(j, offs)
        offs = offs + block_sum
        block_sum = log_weights(j - 1, False)
        done = (jnp.min(offs - slack) > UNDERFLOW_MARGIN).astype(jnp.int32)
        return n + 1, offs, block_sum, done

    acc_ref[...] = jnp.zeros_like(acc_ref)
    carry = (jnp.int32(0), jnp.zeros((tq, 1), F32), log_weights(diag, True), jnp.int32(0))
    n, offs, _, done = lax.while_loop(not_done, step, carry)

    @pl.when(done == 0)
    def _():
        accumulate(diag - n, offs)

    o_ref[...] = acc_ref[...].astype(o_ref.dtype)


def _stickbreak(qkv, n_heads, tq, tk):
    s = qkv.shape[0]
    d = ATT_HEAD_DIM
    tq, tk = min(tq, s), min(tk, s)
    assert s % tk == 0 and tk % tq == 0, (s, tq, tk)
    return pl.pallas_call(
        _stickbreak_body,
        out_shape=jax.ShapeDtypeStruct((s, n_heads * d), BF16),
        grid=(n_heads, s // tq),
        in_specs=[pl.BlockSpec((tq, d), lambda h, i: (i, h)),
                  pl.BlockSpec((s, d), lambda h, i: (0, n_heads + h)),
                  pl.BlockSpec((s, d), lambda h, i: (0, 2 * n_heads + h))],
        out_specs=pl.BlockSpec((tq, d), lambda h, i: (i, h)),
        scratch_shapes=[pltpu.VMEM((tq, d), F32),
                        pltpu.VMEM((tq, tk), F32),
                        pltpu.VMEM((SUBLANES, LANES), F32)],
        compiler_params=_params(2),
        name="stickbreak_attn",
    )(qkv, qkv, qkv)


def _fox_gate_body(fg_ref, bf_ref, cum_ref, carry_ref):
    t = fg_ref.shape[0]

    @pl.when(pl.program_id(0) == 0)
    def _():
        carry_ref[...] = jnp.zeros_like(carry_ref)

    x = fg_ref[...] + bf_ref[...]
    log_f = jnp.minimum(x, 0.0) - jnp.log1p(jnp.exp(-jnp.abs(x)))
    r = lax.broadcasted_iota(jnp.int32, (t, t), 0)
    c = lax.broadcasted_iota(jnp.int32, (t, t), 1)
    tri = jnp.where(c <= r, 1.0, 0.0).astype(BF16)
    cum = _dot3(tri, log_f) + carry_ref[...]
    cum_ref[...] = cum
    carry_ref[...] = cum[t - 1:t, :]


def _fox_gates(fg, bf, t_block):
    s, w = fg.shape
    t_block = min(t_block, s)
    return pl.pallas_call(
        _fox_gate_body,
        out_shape=jax.ShapeDtypeStruct((s, w), F32),
        grid=(s // t_block,),
        in_specs=[pl.BlockSpec((t_block, w), lambda i: (i, 0)),
                  pl.BlockSpec((1, w), lambda i: (0, 0))],
        out_specs=pl.BlockSpec((t_block, w), lambda i: (i, 0)),
        scratch_shapes=[pltpu.VMEM((1, w), F32)],
        compiler_params=_params(1),
        name="fox_gates",
    )(fg, bf)


def _fox_body(q_ref, k_ref, v_ref, cq_ref, ck_ref, o_ref, acc_ref, z_ref, l_ref, knorm_ref):
    tq, d = q_ref.shape
    tk = z_ref.shape[1]
    first_row = pl.program_id(1) * tq
    diag = first_row // tk
    q = q_ref[...] * d ** -0.5
    cq = cq_ref[...]

    def logits(j, diagonal):
        rows = pl.ds(pl.multiple_of(j * tk, tk), tk)
        z = lax.dot_general(q, k_ref[rows, :], (((1,), (1,)), ((), ())),
                            preferred_element_type=F32)
        z = (z + cq) - ck_ref[:, rows]
        if diagonal:
            ahead = (lax.broadcasted_iota(jnp.int32, (tq, tk), 1)
                     - lax.broadcasted_iota(jnp.int32, (tq, tk), 0))
            z = jnp.where(ahead <= first_row - j * tk, z, MASK_VALUE)
        z_ref[...] = z
        return jnp.max(z, axis=-1, keepdims=True)

    def accumulate(j, m, l, row_max):
        rows = pl.ds(pl.multiple_of(j * tk, tk), tk)
        m_new = jnp.maximum(m, row_max)
        scale = jnp.exp(m - m_new)
        p = jnp.exp(z_ref[...] - m_new)
        l = scale * l + jnp.sum(p, axis=-1, keepdims=True)
        acc_ref[...] = scale * acc_ref[...] + jnp.dot(
            p.astype(BF16), v_ref[rows, :], preferred_element_type=F32)
        return m_new, l

    @pl.when(pl.program_id(1) == 0)
    def _():
        knorm_ref[...] = jnp.broadcast_to(_max_sq_row_norm(k_ref, tk), knorm_ref.shape)

    qf = q.astype(F32)
    q_sq = jnp.sum(qf * qf, axis=-1, keepdims=True)
    reach = jnp.sqrt(q_sq * knorm_ref[0:1, 0:1]) + cq

    def not_done(carry):
        n, _, _, _, done = carry
        return jnp.logical_and(n < diag, done == 0)

    def step(carry):
        n, m, l, row_max, _ = carry
        j = diag - n
        m, l = accumulate(j, m, l, row_max)
        row_max = logits(j - 1, False)
        ck_end = jnp.min(ck_ref[:, pl.ds(pl.multiple_of(j * tk - LANES, LANES), LANES)])
        done = (jnp.min(m - reach) + ck_end > UNDERFLOW_MARGIN).astype(jnp.int32)
        return n + 1, m, l, row_max, done

    acc_ref[...] = jnp.zeros_like(acc_ref)
    carry = (jnp.int32(0), jnp.full((tq, 1), MASK_VALUE, F32), jnp.zeros((tq, 1), F32),
             logits(diag, True), jnp.int32(0))
    n, m, l, row_max, done = lax.while_loop(not_done, step, carry)
    l_ref[...] = l

    @pl.when(done == 0)
    def _():
        _, l_last = accumulate(diag - n, m, l, row_max)
        l_ref[...] = l_last

    o_ref[...] = (acc_ref[...] / l_ref[...]).astype(o_ref.dtype)


def _fox(qkv, cum_col, cum_row, n_heads, tq, tk):
    s = qkv.shape[0]
    d = ATT_HEAD_DIM
    tq, tk = min(tq, s), min(tk, s)
    assert s % tk == 0 and tk % tq == 0, (s, tq, tk)
    return pl.pallas_call(
        _fox_body,
        out_shape=jax.ShapeDtypeStruct((s, n_heads * d), BF16),
        grid=(n_heads, s // tq),
        in_specs=[pl.BlockSpec((tq, d), lambda h, i: (i, h)),
                  pl.BlockSpec((s, d), lambda h, i: (0, n_heads + h)),
                  pl.BlockSpec((s, d), lambda h, i: (0, 2 * n_heads + h)),
                  pl.BlockSpec((None, tq, 1), lambda h, i: (h, i, 0)),
                  pl.BlockSpec((None, 1, s), lambda h, i: (h, 0, 0))],
        out_specs=pl.BlockSpec((tq, d), lambda h, i: (i, h)),
        scratch_shapes=[pltpu.VMEM((tq, d), F32),
                        pltpu.VMEM((tq, tk), F32),
                        pltpu.VMEM((tq, 1), F32),
                        pltpu.VMEM((SUBLANES, LANES), F32)],
        compiler_params=_params(2),
        name="fox_attn",
    )(qkv, qkv, qkv, cum_col, cum_row)


def _conv_outproj_ln_body(bg_ref, cg_ref, hd_ref, cgp_ref, hdp_ref, cw_ref, w_ref, h_ref,
                          g_ref, b_ref, of_ref, ob_ref):
    i = pl.program_id(0)
    u = cg_ref[...] * hd_ref[...]
    u_prev = jnp.where(i > 0, cgp_ref[...] * hdp_ref[...], 0.0)
    row8 = lax.broadcasted_iota(jnp.int32, u_prev.shape, 0)
    y = cw_ref[CONV_WIDTH - 1:CONV_WIDTH, :] * u
    for lag in range(1, CONV_WIDTH):
        shifted = pltpu.roll(u, lag, 0)
        head = jnp.where(row8 < lag, pltpu.roll(u_prev, lag, 0), shifted[:SUBLANES])
        shifted = jnp.concatenate([head, shifted[SUBLANES:]], axis=0)
        y += cw_ref[CONV_WIDTH - 1 - lag:CONV_WIDTH - lag, :] * shifted
    acc = jnp.dot((bg_ref[...] * y).astype(BF16), w_ref[...], preferred_element_type=F32)
    out = _layer_norm(ALPHA * h_ref[...] + acc, g_ref[...], b_ref[...])
    of_ref[...] = out
    ob_ref[...] = out.astype(BF16)


def _conv_outproj_ln(proj, conv_w, w, h, g, b, tm):
    m, d = h.shape
    tm = min(tm, m)
    blocks_per_tile = tm // SUBLANES
    col = lambda off: (lambda i: (i, off))
    prev = lambda off: (lambda i: (jnp.maximum(i * blocks_per_tile - 1, 0), off))
    row = lambda i: (i, 0)
    fixed = lambda i: (0, 0)
    return pl.pallas_call(
        _conv_outproj_ln_body,
        out_shape=(jax.ShapeDtypeStruct((m, d), F32),
                   jax.ShapeDtypeStruct((m, d), BF16)),
        grid=(m // tm,),
        in_specs=[pl.BlockSpec((tm, d), col(0)),
                  pl.BlockSpec((tm, d), col(1)),
                  pl.BlockSpec((tm, d), col(2)),
                  pl.BlockSpec((SUBLANES, d), prev(1)),
                  pl.BlockSpec((SUBLANES, d), prev(2)),
                  pl.BlockSpec((CONV_WIDTH, d), fixed),
                  pl.BlockSpec((d, d), fixed),
                  pl.BlockSpec((tm, d), row),
                  pl.BlockSpec((1, d), fixed),
                  pl.BlockSpec((1, d), fixed)],
        out_specs=(pl.BlockSpec((tm, d), row), pl.BlockSpec((tm, d), row)),
        compiler_params=_params(1),
        name="conv_outproj_ln",
    )(proj, proj, proj, proj, proj, conv_w, w, h, g.reshape(1, d), b.reshape(1, d))


def _tiles(s):
    return dict(
        proj_tm=min(1024, s), proj_tn=1024,
        out_tm=min(512, s), conv_tm=min(256, s),
        mlp_tm=min(1024, s), mlp_tf=512,
        hgrn_t=min(256, s), hgrn_heads=4,
        sb_tq=min(1024, s), sb_tk=min(1024, s), fox_tq=min(1024, s), fox_tk=min(1024, s),
        gate_t=min(256, s),
    )


def kernel(x, w_mix_a, norm_g_a, lb_logits, w_out_a, w_mix_b, w_out_b, w_mix_c, b_f_c,
           w_out_c, w_mix_d, conv_w_d, w_out_d, ln_mix_g, ln_mix_b, w_ff1, w_ff2,
           ln_ff_g, ln_ff_b):
    bsz, s, d = x.shape
    n_att_heads = d // ATT_HEAD_DIM
    tl = _tiles(s)
    outs = []
    for bi in range(bsz):
        h = x[bi]
        hb = h.astype(BF16)
        for i in range(DEPTH):
            m, j = i % N_MIXERS, i // N_MIXERS
            g, b = ln_mix_g[i], ln_mix_b[i]
            if m == 0:
                proj = _proj(hb, w_mix_a[j].astype(BF16), F32, tl["proj_tm"], tl["proj_tn"])
                y = _hgrn(proj, lb_logits, norm_g_a[j], i, tl["hgrn_t"], tl["hgrn_heads"])
                h, hb = _outproj_ln(y, w_out_a[j].astype(BF16), h, g, b, tl["out_tm"])
            elif m == 1:
                qkv = _proj(hb, w_mix_b[j].astype(BF16), BF16, tl["proj_tm"], tl["proj_tn"])
                y = _stickbreak(qkv, n_att_heads, tl["sb_tq"], tl["sb_tk"])
                h, hb = _outproj_ln(y, w_out_b[j].astype(BF16), h, g, b, tl["out_tm"])
            elif m == 2:
                wc = w_mix_c[j]
                qkv = _proj(hb, wc[:, :3 * d].astype(BF16), BF16, tl["proj_tm"], tl["proj_tn"])
                lane_pad = HGRN_HEAD_DIM - n_att_heads
                wf = jnp.pad(wc[:, 3 * d:], ((0, 0), (0, lane_pad))).astype(BF16)
                fg = _proj(hb, wf, F32, tl["proj_tm"], tl["proj_tn"])
                bf = jnp.pad(b_f_c[j], (0, lane_pad)).reshape(1, -1)
                cum = _fox_gates(fg, bf, tl["gate_t"])[:, :n_att_heads]
                cum_t = cum.T
                y = _fox(qkv, cum_t[:, :, None], cum_t[:, None, :], n_att_heads,
                         tl["fox_tq"], tl["fox_tk"])
                h, hb = _outproj_ln(y, w_out_c[j].astype(BF16), h, g, b, tl["out_tm"])
            else:
                proj = _proj(hb, w_mix_d[j].astype(BF16), F32, tl["proj_tm"], tl["proj_tn"])
                h, hb = _conv_outproj_ln(proj, conv_w_d[j], w_out_d[j].astype(BF16), h, g, b,
                                         tl["conv_tm"])
            h, hb = _mlp_ln(hb, h, w_ff1[i].astype(BF16), w_ff2[i].astype(BF16),
                            ln_ff_g[i], ln_ff_b[i], tl["mlp_tm"], tl["mlp_tf"])
        outs.append(h)
    return outs[0][None] if bsz == 1 else jnp.stack(outs, axis=0)
```

```python
import functools

import jax
import jax.numpy as jnp
from jax import lax
from jax.experimental import pallas as pl
from jax.experimental.pallas import tpu as pltpu

DEPTH = 4
N_MIXERS = 4
HGRN_HEAD_DIM = 128
ATT_HEAD_DIM = 256
CHUNK = 16
CONV_WIDTH = 3
LN_EPS = 1e-5
RMS_EPS = 1e-6
ALPHA = (2.0 * DEPTH) ** 0.25
MASK_VALUE = -1e30
UNDERFLOW_MARGIN = 100.0
LOG2E = 1.4426950408889634
V7X_MXU_WIDTH = 256
SB_CUMSUM_WIDTH = V7X_MXU_WIDTH

V7X_VMEM_LIMIT_BYTES = 56 * 1024 * 1024
SUBLANES = 8
LANES = 128

F32 = jnp.float32
BF16 = jnp.bfloat16


def _params(n_grid_axes, vmem_bytes=V7X_VMEM_LIMIT_BYTES):
    return pltpu.CompilerParams(
        dimension_semantics=("arbitrary",) * n_grid_axes,
        vmem_limit_bytes=vmem_bytes)


def _layer_norm(v, g, b):
    mu = jnp.mean(v, axis=-1, keepdims=True)
    c = v - mu
    var = jnp.mean(c * c, axis=-1, keepdims=True)
    return c * lax.rsqrt(var + LN_EPS) * g + b


def _split3(x):
    hi = x.astype(BF16)
    r1 = x - hi.astype(F32)
    mid = r1.astype(BF16)
    lo = (r1 - mid.astype(F32)).astype(BF16)
    return hi, mid, lo


def _dot3(mat01, x):
    hi, mid, lo = _split3(x)
    out = jnp.dot(mat01, lo, preferred_element_type=F32)
    out += jnp.dot(mat01, mid, preferred_element_type=F32)
    out += jnp.dot(mat01, hi, preferred_element_type=F32)
    return out


def _softplus(z):
    return jnp.maximum(z, 0.0) + jnp.log1p(jnp.exp(-jnp.abs(z)))


def _proj_body(x_ref, w_ref, o_ref, xb_ref):
    @pl.when(pl.program_id(1) == 0)
    def _():
        xb_ref[...] = x_ref[...].astype(BF16)

    o_ref[...] = jnp.dot(xb_ref[...], w_ref[...],
                         preferred_element_type=F32).astype(o_ref.dtype)


def _proj(x, w, out_dtype, tm, tn, n=None):
    m, k = x.shape
    n = w.shape[1] if n is None else n
    tm, tn = min(tm, m), min(tn, n)
    assert m % tm == 0 and n % tn == 0, (m, n, tm, tn)
    return pl.pallas_call(
        _proj_body,
        out_shape=jax.ShapeDtypeStruct((m, n), out_dtype),
        grid=(m // tm, n // tn),
        in_specs=[pl.BlockSpec((tm, k), lambda i, j: (i, 0)),
                  pl.BlockSpec((k, tn), lambda i, j: (0, j))],
        out_specs=pl.BlockSpec((tm, tn), lambda i, j: (i, j)),
        scratch_shapes=[pltpu.VMEM((tm, k), BF16)],
        compiler_params=_params(2),
        name="proj",
    )(x, w)


def _outproj_ln_body(y_ref, w_ref, h_ref, g_ref, b_ref, of_ref, ob_ref):
    acc = jnp.dot(y_ref[...], w_ref[...], preferred_element_type=F32)
    out = _layer_norm(ALPHA * h_ref[...] + acc, g_ref[...], b_ref[...])
    of_ref[...] = out
    ob_ref[...] = out.astype(BF16)


def _outproj_ln(y, w, h, g, b, tm):
    m, d = h.shape
    tm = min(tm, m)
    row = lambda i: (i, 0)
    fixed = lambda i: (0, 0)
    return pl.pallas_call(
        _outproj_ln_body,
        out_shape=(jax.ShapeDtypeStruct((m, d), F32),
                   jax.ShapeDtypeStruct((m, d), BF16)),
        grid=(m // tm,),
        in_specs=[pl.BlockSpec((tm, d), row),
                  pl.BlockSpec((d, d), fixed),
                  pl.BlockSpec((tm, d), row),
                  pl.BlockSpec((1, d), fixed),
                  pl.BlockSpec((1, d), fixed)],
        out_specs=(pl.BlockSpec((tm, d), row), pl.BlockSpec((tm, d), row)),
        compiler_params=_params(1),
        name="outproj_ln",
    )(y, w, h, g.reshape(1, d), b.reshape(1, d))


def _mlp_ln_body(xb_ref, h_ref, w1_ref, w2_ref, g_ref, b_ref, of_ref):
    j = pl.program_id(1)

    @pl.when(j == 0)
    def _():
        of_ref[...] = ALPHA * h_ref[...]

    a = jnp.dot(xb_ref[...], w1_ref[...], preferred_element_type=F32)
    a = jnp.maximum(a, 0.0)
    of_ref[...] += jnp.dot((a * a).astype(BF16), w2_ref[...], preferred_element_type=F32)

    @pl.when(j == pl.num_programs(1) - 1)
    def _():
        of_ref[...] = _layer_norm(of_ref[...], g_ref[...], b_ref[...])


def _mlp_ln(xb, h, w1, w2, g, b, tm, tf):
    m, d = h.shape
    f = w1.shape[1]
    tm, tf = min(tm, m), min(tf, f)
    row = lambda i, j: (i, 0)
    fixed = lambda i, j: (0, 0)
    return pl.pallas_call(
        _mlp_ln_body,
        out_shape=jax.ShapeDtypeStruct((m, d), F32),
        grid=(m // tm, f // tf),
        in_specs=[pl.BlockSpec((tm, d), row),
                  pl.BlockSpec((tm, d), row),
                  pl.BlockSpec((d, tf), lambda i, j: (0, j)),
                  pl.BlockSpec((tf, d), lambda i, j: (j, 0)),
                  pl.BlockSpec((1, d), fixed),
                  pl.BlockSpec((1, d), fixed)],
        out_specs=pl.BlockSpec((tm, d), row),
        compiler_params=_params(2),
        name="mlp_ln",
    )(xb, h, w1, w2, g.reshape(1, d), b.reshape(1, d))


def _hgrn_body(layer, n_heads, q_ref, f_ref, i_ref, g_ref, lbl_ref, ng_ref, o_ref,
               st_ref, qd_ref, kd_ref, v_ref, dl_ref, acc_ref):
    t = q_ref.shape[0]
    dk = HGRN_HEAD_DIM

    @pl.when(pl.program_id(1) == 0)
    def _():
        st_ref[...] = jnp.zeros_like(st_ref)

    lg = lbl_ref[...]
    e = jnp.exp(lg - jnp.max(lg, axis=0, keepdims=True))
    lb = jnp.sum(e[:layer + 1], axis=0, keepdims=True) / jnp.sum(e, axis=0, keepdims=True)

    f = f_ref[...]
    e = jnp.exp(-jnp.abs(f))
    log_sig = jnp.minimum(f, 0.0) - jnp.log(1.0 + e)
    a = jnp.log(lb)
    b = jnp.log1p(-lb) + log_sig
    log_f = jnp.maximum(a, b) + jnp.log(1.0 + jnp.exp(-jnp.abs(a - b)))
    k = (1.0 - lb) * (jnp.where(f >= 0.0, e, 1.0) / (1.0 + e))
    q = q_ref[...]
    q = q * jax.nn.sigmoid(q) * dk ** -0.5

    r = lax.broadcasted_iota(jnp.int32, (t, t), 0)
    c = lax.broadcasted_iota(jnp.int32, (t, t), 1)
    same_chunk = (r // CHUNK) == (c // CHUNK)
    causal = same_chunk & (c <= r)
    tri = jnp.where(causal, 1.0, 0.0).astype(BF16)
    ones = jnp.where(same_chunk, 1.0, 0.0).astype(BF16)
    bc = _dot3(tri, log_f)
    b_last = _dot3(ones, log_f)

    qd_ref[...] = (q * jnp.exp(bc)).astype(BF16)
    k_inv = (k * jnp.exp(-bc)).astype(BF16)
    kd_ref[...] = k * jnp.exp(b_last - bc)
    dl_ref[...] = jnp.exp(b_last)
    v_ref[...] = i_ref[...].astype(BF16)

    for h in range(n_heads):
        cols = slice(h * dk, (h + 1) * dk)
        sc = lax.dot_general(qd_ref[:, cols], k_inv[:, cols], (((1,), (1,)), ((), ())),
                             preferred_element_type=F32)
        sc = jnp.where(causal, sc, 0.0).astype(BF16)
        acc_ref[:, cols] = jnp.dot(sc, v_ref[:, cols], preferred_element_type=F32)

    n_chunks = t // CHUNK
    chunk_of_row = lax.broadcasted_iota(jnp.int32, (t, dk), 0) // CHUNK
    for h in range(n_heads):
        cols = slice(h * dk, (h + 1) * dk)
        kd = kd_ref[:, cols]
        kd_by_chunk = jnp.concatenate(
            [jnp.where(chunk_of_row == ci, kd, 0.0) for ci in range(n_chunks)], axis=1)
        upd = lax.dot_general(v_ref[:, cols], kd_by_chunk.astype(BF16),
                              (((0,), (0,)), ((), ())), preferred_element_type=F32)
        st = st_ref[h]
        for ci in range(n_chunks):
            rows = slice(ci * CHUNK, (ci + 1) * CHUNK)
            inter = lax.dot_general(qd_ref[rows, cols], st.astype(BF16),
                                    (((1,), (1,)), ((), ())), preferred_element_type=F32)
            acc_ref[rows, cols] += inter
            st = dl_ref[ci * CHUNK:ci * CHUNK + 1, cols] * st + upd[:, ci * dk:(ci + 1) * dk]
        st_ref[h] = st

    ng = ng_ref[...]
    g = g_ref[...]
    gate = g * jax.nn.sigmoid(g)
    for h in range(n_heads):
        cols = slice(h * dk, (h + 1) * dk)
        o = acc_ref[:, cols]
        o = o * lax.rsqrt(jnp.mean(o * o, axis=-1, keepdims=True) + RMS_EPS) * ng
        o_ref[:, cols] = (o * gate[:, cols]).astype(BF16)


def _hgrn(proj, lb_logits, norm_g, layer, t_block, heads_per_step):
    s, d4 = proj.shape
    d = d4 // 4
    t_block = min(t_block, s)
    gw = heads_per_step * HGRN_HEAD_DIM
    n_groups = d // gw
    n_lb = lb_logits.shape[0]
    col = lambda off: (lambda hg, tb: (tb, off * n_groups + hg))
    return pl.pallas_call(
        functools.partial(_hgrn_body, layer, heads_per_step),
        out_shape=jax.ShapeDtypeStruct((s, d), BF16),
        grid=(n_groups, s // t_block),
        in_specs=[pl.BlockSpec((t_block, gw), col(0)),
                  pl.BlockSpec((t_block, gw), col(1)),
                  pl.BlockSpec((t_block, gw), col(2)),
                  pl.BlockSpec((t_block, gw), col(3)),
                  pl.BlockSpec((n_lb, gw), lambda hg, tb: (0, hg)),
                  pl.BlockSpec((1, HGRN_HEAD_DIM), lambda hg, tb: (0, 0))],
        out_specs=pl.BlockSpec((t_block, gw), lambda hg, tb: (tb, hg)),
        scratch_shapes=[pltpu.VMEM((heads_per_step, HGRN_HEAD_DIM, HGRN_HEAD_DIM), F32),
                        pltpu.VMEM((t_block, gw), BF16),
                        pltpu.VMEM((t_block, gw), F32),
                        pltpu.VMEM((t_block, gw), BF16),
                        pltpu.VMEM((t_block, gw), F32),
                        pltpu.VMEM((t_block, gw), F32)],
        compiler_params=_params(2),
        name="hgrn2",
    )(proj, proj, proj, proj, lb_logits, norm_g.reshape(1, HGRN_HEAD_DIM))


def _max_sq_row_norm(ref, chunk):
    def body(c, best):
        x = ref[pl.ds(pl.multiple_of(c * chunk, chunk), chunk), :].astype(F32)
        sq = jnp.sum(x * x, axis=-1, keepdims=True)
        return jnp.maximum(best, jnp.max(sq, axis=0, keepdims=True))
    return lax.fori_loop(0, ref.shape[0] // chunk, body, jnp.zeros((1, 1), F32))


def _stickbreak_body(q_ref, k_ref, v_ref, o_ref, acc_ref, lw_ref, knorm_ref):
    tq, d = q_ref.shape
    tk = lw_ref.shape[1]
    cw = min(SB_CUMSUM_WIDTH, tk)
    first_row = pl.program_id(1) * tq
    diag = first_row // tk
    q = q_ref[...] * d ** -0.5
    r = lax.broadcasted_iota(jnp.int32, (cw, cw), 0)
    c = lax.broadcasted_iota(jnp.int32, (cw, cw), 1)
    incl_rev = jnp.where(r >= c, 1.0, 0.0).astype(BF16)

    def log_weights(j, diagonal):
        rows = pl.ds(pl.multiple_of(j * tk, tk), tk)
        z = lax.dot_general(q, k_ref[rows, :], (((1,), (1,)), ((), ())),
                            preferred_element_type=F32)
        if diagonal:
            ahead = (lax.broadcasted_iota(jnp.int32, (tq, tk), 1)
                     - lax.broadcasted_iota(jnp.int32, (tq, tk), 0))
            z = jnp.where(ahead < first_row - j * tk, z, MASK_VALUE)
        sp = jnp.maximum(z, 0.0) + jnp.log(1.0 + jnp.exp2(jnp.abs(z) * -LOG2E))
        sp = sp.astype(BF16)
        later = None
        for sub in reversed(range(tk // cw)):
            cols = slice(sub * cw, (sub + 1) * cw)
            within = jnp.dot(sp[:, cols], incl_rev, preferred_element_type=F32)
            total = within if later is None else within + later
            lw_ref[:, cols] = z[:, cols] - total
            later = total[:, 0:1]
        return later

    def accumulate(j, offs):
        rows = pl.ds(pl.multiple_of(j * tk, tk), tk)
        w = jnp.exp(lw_ref[...] - offs)
        acc_ref[...] += jnp.dot(w.astype(BF16), v_ref[rows, :], preferred_element_type=F32)

    @pl.when(pl.program_id(1) == 0)
    def _():
        knorm_ref[...] = jnp.broadcast_to(_max_sq_row_norm(k_ref, tk), knorm_ref.shape)

    qf = q.astype(F32)
    q_sq = jnp.max(jnp.sum(qf * qf, axis=-1, keepdims=True), axis=0, keepdims=True)
    slack = jnp.sqrt(q_sq * knorm_ref[0:1, 0:1]) * 2.0 ** -8

    def not_done(carry):
        n, _, _, done = carry
        return jnp.logical_and(n < diag, done == 0)

    def step(carry):
        n, offs, block_sum, _ = carry
        j = diag - n
        accumulate(j, offs)
        offs = offs + block_sum
        block_sum = log_weights(j - 1, False)
        done = (jnp.min(offs - slack) > UNDERFLOW_MARGIN).astype(jnp.int32)
        return n + 1, offs, block_sum, done

    acc_ref[...] = jnp.zeros_like(acc_ref)
    carry = (jnp.int32(0), jnp.zeros((tq, 1), F32), log_weights(diag, True), jnp.int32(0))
    n, offs, _, done = lax.while_loop(not_done, step, carry)

    @pl.when(done == 0)
    def _():
        accumulate(diag - n, offs)

    o_ref[...] = acc_ref[...].astype(o_ref.dtype)


def _stickbreak(qkv, n_heads, tq, tk):
    s = qkv.shape[0]
    d = ATT_HEAD_DIM
    tq, tk = min(tq, s), min(tk, s)
    assert s % tk == 0 and tk % tq == 0, (s, tq, tk)
    return pl.pallas_call(
        _stickbreak_body,
        out_shape=jax.ShapeDtypeStruct((s, n_heads * d), BF16),
        grid=(n_heads, s // tq),
        in_specs=[pl.BlockSpec((tq, d), lambda h, i: (i, h)),
                  pl.BlockSpec((s, d), lambda h, i: (0, n_heads + h)),
                  pl.BlockSpec((s, d), lambda h, i: (0, 2 * n_heads + h))],
        out_specs=pl.BlockSpec((tq, d), lambda h, i: (i, h)),
        scratch_shapes=[pltpu.VMEM((tq, d), F32),
                        pltpu.VMEM((tq, tk), F32),
                        pltpu.VMEM((SUBLANES, LANES), F32)],
        compiler_params=_params(2),
        name="stickbreak_attn",
    )(qkv, qkv, qkv)


def _fox_gate_body(fg_ref, bf_ref, cum_ref, carry_ref):
    t = fg_ref.shape[0]

    @pl.when(pl.program_id(0) == 0)
    def _():
        carry_ref[...] = jnp.zeros_like(carry_ref)

    x = fg_ref[...] + bf_ref[...]
    log_f = jnp.minimum(x, 0.0) - jnp.log1p(jnp.exp(-jnp.abs(x)))
    r = lax.broadcasted_iota(jnp.int32, (t, t), 0)
    c = lax.broadcasted_iota(jnp.int32, (t, t), 1)
    tri = jnp.where(c <= r, 1.0, 0.0).astype(BF16)
    cum = _dot3(tri, log_f) + carry_ref[...]
    cum_ref[...] = cum
    carry_ref[...] = cum[t - 1:t, :]


def _fox_gates(fg, bf, t_block):
    s, w = fg.shape
    t_block = min(t_block, s)
    return pl.pallas_call(
        _fox_gate_body,
        out_shape=jax.ShapeDtypeStruct((s, w), F32),
        grid=(s // t_block,),
        in_specs=[pl.BlockSpec((t_block, w), lambda i: (i, 0)),
                  pl.BlockSpec((1, w), lambda i: (0, 0))],
        out_specs=pl.BlockSpec((t_block, w), lambda i: (i, 0)),
        scratch_shapes=[pltpu.VMEM((1, w), F32)],
        compiler_params=_params(1),
        name="fox_gates",
    )(fg, bf)


def _fox_body(q_ref, k_ref, v_ref, cq_ref, ck_ref, o_ref, acc_ref, z_ref, l_ref, knorm_ref):
    tq, d = q_ref.shape
    tk = z_ref.shape[1]
    first_row = pl.program_id(1) * tq
    diag = first_row // tk
    q = q_ref[...] * d ** -0.5
    cq = cq_ref[...]

    def logits(j, diagonal):
        rows = pl.ds(pl.multiple_of(j * tk, tk), tk)
        z = lax.dot_general(q, k_ref[rows, :], (((1,), (1,)), ((), ())),
                            preferred_element_type=F32)
        z = (z + cq) - ck_ref[:, rows]
        if diagonal:
            ahead = (lax.broadcasted_iota(jnp.int32, (tq, tk), 1)
                     - lax.broadcasted_iota(jnp.int32, (tq, tk), 0))
            z = jnp.where(ahead <= first_row - j * tk, z, MASK_VALUE)
        z_ref[...] = z
        return jnp.max(z, axis=-1, keepdims=True)

    def accumulate(j, m, l, row_max):
        rows = pl.ds(pl.multiple_of(j * tk, tk), tk)
        m_new = jnp.maximum(m, row_max)
        scale = jnp.exp(m - m_new)
        p = jnp.exp(z_ref[...] - m_new)
        l = scale * l + jnp.sum(p, axis=-1, keepdims=True)
        acc_ref[...] = scale * acc_ref[...] + jnp.dot(
            p.astype(BF16), v_ref[rows, :], preferred_element_type=F32)
        return m_new, l

    @pl.when(pl.program_id(1) == 0)
    def _():
        knorm_ref[...] = jnp.broadcast_to(_max_sq_row_norm(k_ref, tk), knorm_ref.shape)

    qf = q.astype(F32)
    q_sq = jnp.sum(qf * qf, axis=-1, keepdims=True)
    reach = jnp.sqrt(q_sq * knorm_ref[0:1, 0:1]) + cq

    def not_done(carry):
        n, _, _, _, done = carry
        return jnp.logical_and(n < diag, done == 0)

    def step(carry):
        n, m, l, row_max, _ = carry
        j = diag - n
        m, l = accumulate(j, m, l, row_max)
        row_max = logits(j - 1, False)
        ck_end = jnp.min(ck_ref[:, pl.ds(pl.multiple_of(j * tk - LANES, LANES), LANES)])
        done = (jnp.min(m - reach) + ck_end > UNDERFLOW_MARGIN).astype(jnp.int32)
        return n + 1, m, l, row_max, done

    acc_ref[...] = jnp.zeros_like(acc_ref)
    carry = (jnp.int32(0), jnp.full((tq, 1), MASK_VALUE, F32), jnp.zeros((tq, 1), F32),
             logits(diag, True), jnp.int32(0))
    n, m, l, row_max, done = lax.while_loop(not_done, step, carry)
    l_ref[...] = l

    @pl.when(done == 0)
    def _():
        _, l_last = accumulate(diag - n, m, l, row_max)
        l_ref[...] = l_last

    o_ref[...] = (acc_ref[...] / l_ref[...]).astype(o_ref.dtype)


def _fox(qkv, cum_col, cum_row, n_heads, tq, tk):
    s = qkv.shape[0]
    d = ATT_HEAD_DIM
    tq, tk = min(tq, s), min(tk, s)
    assert s % tk == 0 and tk % tq == 0, (s, tq, tk)
    return pl.pallas_call(
        _fox_body,
        out_shape=jax.ShapeDtypeStruct((s, n_heads * d), BF16),
        grid=(n_heads, s // tq),
        in_specs=[pl.BlockSpec((tq, d), lambda h, i: (i, h)),
                  pl.BlockSpec((s, d), lambda h, i: (0, n_heads + h)),
                  pl.BlockSpec((s, d), lambda h, i: (0, 2 * n_heads + h)),
                  pl.BlockSpec((None, tq, 1), lambda h, i: (h, i, 0)),
                  pl.BlockSpec((None, 1, s), lambda h, i: (h, 0, 0))],
        out_specs=pl.BlockSpec((tq, d), lambda h, i: (i, h)),
        scratch_shapes=[pltpu.VMEM((tq, d), F32),
                        pltpu.VMEM((tq, tk), F32),
                        pltpu.VMEM((tq, 1), F32),
                        pltpu.VMEM((SUBLANES, LANES), F32)],
        compiler_params=_params(2),
        name="fox_attn",
    )(qkv, qkv, qkv, cum_col, cum_row)


def _conv_outproj_ln_body(bg_ref, cg_ref, hd_ref, cgp_ref, hdp_ref, cw_ref, w_ref, h_ref,
                          g_ref, b_ref, of_ref, ob_ref):
    i = pl.program_id(0)
    u = cg_ref[...] * hd_ref[...]
    u_prev = jnp.where(i > 0, cgp_ref[...] * hdp_ref[...], 0.0)
    row8 = lax.broadcasted_iota(jnp.int32, u_prev.shape, 0)
    y = cw_ref[CONV_WIDTH - 1:CONV_WIDTH, :] * u
    for lag in range(1, CONV_WIDTH):
        shifted = pltpu.roll(u, lag, 0)
        head = jnp.where(row8 < lag, pltpu.roll(u_prev, lag, 0), shifted[:SUBLANES])
        shifted = jnp.concatenate([head, shifted[SUBLANES:]], axis=0)
        y += cw_ref[CONV_WIDTH - 1 - lag:CONV_WIDTH - lag, :] * shifted
    acc = jnp.dot((bg_ref[...] * y).astype(BF16), w_ref[...], preferred_element_type=F32)
    out = _layer_norm(ALPHA * h_ref[...] + acc, g_ref[...], b_ref[...])
    of_ref[...] = out
    ob_ref[...] = out.astype(BF16)


def _conv_outproj_ln(proj, conv_w, w, h, g, b, tm):
    m, d = h.shape
    tm = min(tm, m)
    blocks_per_tile = tm // SUBLANES
    col = lambda off: (lambda i: (i, off))
    prev = lambda off: (lambda i: (jnp.maximum(i * blocks_per_tile - 1, 0), off))
    row = lambda i: (i, 0)
    fixed = lambda i: (0, 0)
    return pl.pallas_call(
        _conv_outproj_ln_body,
        out_shape=(jax.ShapeDtypeStruct((m, d), F32),
                   jax.ShapeDtypeStruct((m, d), BF16)),
        grid=(m // tm,),
        in_specs=[pl.BlockSpec((tm, d), col(0)),
                  pl.BlockSpec((tm, d), col(1)),
                  pl.BlockSpec((tm, d), col(2)),
                  pl.BlockSpec((SUBLANES, d), prev(1)),
                  pl.BlockSpec((SUBLANES, d), prev(2)),
                  pl.BlockSpec((CONV_WIDTH, d), fixed),
                  pl.BlockSpec((d, d), fixed),
                  pl.BlockSpec((tm, d), row),
                  pl.BlockSpec((1, d), fixed),
                  pl.BlockSpec((1, d), fixed)],
        out_specs=(pl.BlockSpec((tm, d), row), pl.BlockSpec((tm, d), row)),
        compiler_params=_params(1),
        name="conv_outproj_ln",
    )(proj, proj, proj, proj, proj, conv_w, w, h, g.reshape(1, d), b.reshape(1, d))


def _tiles(s):
    return dict(
        proj_tm=min(1024, s), proj_tn=1024,
        out_tm=min(512, s), conv_tm=min(256, s),
        mlp_tm=min(1024, s), mlp_tf=512,
        hgrn_t=min(256, s), hgrn_heads=8,
        sb_tq=min(512, s), sb_tk=min(512, s), fox_tq=min(512, s), fox_tk=min(512, s),
        gate_t=min(256, s),
    )


def kernel(x, w_mix_a, norm_g_a, lb_logits, w_out_a, w_mix_b, w_out_b, w_mix_c, b_f_c,
           w_out_c, w_mix_d, conv_w_d, w_out_d, ln_mix_g, ln_mix_b, w_ff1, w_ff2,
           ln_ff_g, ln_ff_b):
    bsz, s, d = x.shape
    n_att_heads = d // ATT_HEAD_DIM
    tl = _tiles(s)
    outs = []
    for bi in range(bsz):
        h = x[bi]
        for i in range(DEPTH):
            m, j = i % N_MIXERS, i // N_MIXERS
            g, b = ln_mix_g[i], ln_mix_b[i]
            if m == 0:
                proj = _proj(h, w_mix_a[j].astype(BF16), F32, tl["proj_tm"], tl["proj_tn"])
                y = _hgrn(proj, lb_logits, norm_g_a[j], i, tl["hgrn_t"], tl["hgrn_heads"])
                h, hb = _outproj_ln(y, w_out_a[j].astype(BF16), h, g, b, tl["out_tm"])
            elif m == 1:
                qkv = _proj(h, w_mix_b[j].astype(BF16), BF16, tl["proj_tm"], tl["proj_tn"])
                y = _stickbreak(qkv, n_att_heads, tl["sb_tq"], tl["sb_tk"])
                h, hb = _outproj_ln(y, w_out_b[j].astype(BF16), h, g, b, tl["out_tm"])
            elif m == 2:
                wc = w_mix_c[j].astype(BF16)
                qkv = _proj(h, wc, BF16, tl["proj_tm"], tl["proj_tn"], n=3 * d)
                lane_pad = HGRN_HEAD_DIM - n_att_heads
                wf = jnp.pad(wc[:, 3 * d:], ((0, 0), (0, lane_pad)))
                fg = _proj(h, wf, F32, tl["proj_tm"], tl["proj_tn"])
                bf = jnp.pad(b_f_c[j], (0, lane_pad)).reshape(1, -1)
                cum = _fox_gates(fg, bf, tl["gate_t"])[:, :n_att_heads]
                cum_t = cum.T
                y = _fox(qkv, cum_t[:, :, None], cum_t[:, None, :], n_att_heads,
                         tl["fox_tq"], tl["fox_tk"])
                h, hb = _outproj_ln(y, w_out_c[j].astype(BF16), h, g, b, tl["out_tm"])
            else:
                proj = _proj(h, w_mix_d[j].astype(BF16), F32, tl["proj_tm"], tl["proj_tn"])
                h, hb = _conv_outproj_ln(proj, conv_w_d[j], w_out_d[j].astype(BF16), h, g, b,
                                         tl["conv_tm"])
            h = _mlp_ln(hb, h, w_ff1[i].astype(BF16), w_ff2[i].astype(BF16),
                        ln_ff_g[i], ln_ff_b[i], tl["mlp_tm"], tl["mlp_tf"])
        outs.append(h)
    return outs[0][None] if bsz == 1 else jnp.stack(outs, axis=0)
```

```python
import functools

import jax
import jax.numpy as jnp
from jax import lax
from jax.experimental import pallas as pl
from jax.experimental.pallas import tpu as pltpu

DEPTH = 4
N_MIXERS = 4
HGRN_HEAD_DIM = 128
ATT_HEAD_DIM = 256
CHUNK = 16
CONV_WIDTH = 3
LN_EPS = 1e-5
RMS_EPS = 1e-6
ALPHA = (2.0 * DEPTH) ** 0.25
MASK_VALUE = -1e30
UNDERFLOW_MARGIN = 100.0
LOG2E = 1.4426950408889634
V7X_MXU_WIDTH = 256
OUTPROJ_ROW_STRIP = 128
SB_CUMSUM_WIDTH = V7X_MXU_WIDTH

V7X_VMEM_LIMIT_BYTES = 56 * 1024 * 1024
SUBLANES = 8
LANES = 128

F32 = jnp.float32
BF16 = jnp.bfloat16


def _params(n_grid_axes, vmem_bytes=V7X_VMEM_LIMIT_BYTES):
    return pltpu.CompilerParams(
        dimension_semantics=("arbitrary",) * n_grid_axes,
        vmem_limit_bytes=vmem_bytes)


def _layer_norm(v, g, b):
    mu = jnp.mean(v, axis=-1, keepdims=True)
    c = v - mu
    var = jnp.mean(c * c, axis=-1, keepdims=True)
    return c * lax.rsqrt(var + LN_EPS) * g + b


def _split3(x):
    hi = x.astype(BF16)
    r1 = x - hi.astype(F32)
    mid = r1.astype(BF16)
    lo = (r1 - mid.astype(F32)).astype(BF16)
    return hi, mid, lo


def _dot3(mat01, x):
    hi, mid, lo = _split3(x)
    out = jnp.dot(mat01, lo, preferred_element_type=F32)
    out += jnp.dot(mat01, mid, preferred_element_type=F32)
    out += jnp.dot(mat01, hi, preferred_element_type=F32)
    return out


def _softplus(z):
    return jnp.maximum(z, 0.0) + jnp.log1p(jnp.exp(-jnp.abs(z)))


def _proj_body(x_ref, w_ref, o_ref, xb_ref):
    @pl.when(pl.program_id(1) == 0)
    def _():
        xb_ref[...] = x_ref[...].astype(BF16)

    o_ref[...] = jnp.dot(xb_ref[...], w_ref[...],
                         preferred_element_type=F32).astype(o_ref.dtype)


def _proj(x, w, out_dtype, tm, tn, n=None):
    m, k = x.shape
    n = w.shape[1] if n is None else n
    tm, tn = min(tm, m), min(tn, n)
    assert m % tm == 0 and n % tn == 0, (m, n, tm, tn)
    return pl.pallas_call(
        _proj_body,
        out_shape=jax.ShapeDtypeStruct((m, n), out_dtype),
        grid=(m // tm, n // tn),
        in_specs=[pl.BlockSpec((tm, k), lambda i, j: (i, 0)),
                  pl.BlockSpec((k, tn), lambda i, j: (0, j))],
        out_specs=pl.BlockSpec((tm, tn), lambda i, j: (i, j)),
        scratch_shapes=[pltpu.VMEM((tm, k), BF16)],
        compiler_params=_params(2),
        name="proj",
    )(x, w)


def _outproj_ln_body(y_ref, w_ref, h_ref, g_ref, b_ref, of_ref, ob_ref):
    tm = y_ref.shape[0]
    strip = min(OUTPROJ_ROW_STRIP, tm)
    for r0 in range(0, tm, strip):
        rows = slice(r0, r0 + strip)
        acc = jnp.dot(y_ref[rows, :], w_ref[...], preferred_element_type=F32)
        out = _layer_norm(ALPHA * h_ref[rows, :] + acc, g_ref[...], b_ref[...])
        of_ref[rows, :] = out
        ob_ref[rows, :] = out.astype(BF16)


def _outproj_ln(y, w, h, g, b, tm):
    m, d = h.shape
    tm = min(tm, m)
    row = lambda i: (i, 0)
    fixed = lambda i: (0, 0)
    return pl.pallas_call(
        _outproj_ln_body,
        out_shape=(jax.ShapeDtypeStruct((m, d), F32),
                   jax.ShapeDtypeStruct((m, d), BF16)),
        grid=(m // tm,),
        in_specs=[pl.BlockSpec((tm, d), row),
                  pl.BlockSpec((d, d), fixed),
                  pl.BlockSpec((tm, d), row),
                  pl.BlockSpec((1, d), fixed),
                  pl.BlockSpec((1, d), fixed)],
        out_specs=(pl.BlockSpec((tm, d), row), pl.BlockSpec((tm, d), row)),
        compiler_params=_params(1),
        name="outproj_ln",
    )(y, w, h, g.reshape(1, d), b.reshape(1, d))


def _mlp_ln_body(xb_ref, h_ref, w1_ref, w2_ref, g_ref, b_ref, of_ref):
    j = pl.program_id(1)

    @pl.when(j == 0)
    def _():
        of_ref[...] = ALPHA * h_ref[...]

    a = jnp.dot(xb_ref[...], w1_ref[...], preferred_element_type=F32)
    a = jnp.maximum(a, 0.0)
    of_ref[...] += jnp.dot((a * a).astype(BF16), w2_ref[...], preferred_element_type=F32)

    @pl.when(j == pl.num_programs(1) - 1)
    def _():
        of_ref[...] = _layer_norm(of_ref[...], g_ref[...], b_ref[...])


def _mlp_ln(xb, h, w1, w2, g, b, tm, tf):
    m, d = h.shape
    f = w1.shape[1]
    tm, tf = min(tm, m), min(tf, f)
    row = lambda i, j: (i, 0)
    fixed = lambda i, j: (0, 0)
    return pl.pallas_call(
        _mlp_ln_body,
        out_shape=jax.ShapeDtypeStruct((m, d), F32),
        grid=(m // tm, f // tf),
        in_specs=[pl.BlockSpec((tm, d), row),
                  pl.BlockSpec((tm, d), row),
                  pl.BlockSpec((d, tf), lambda i, j: (0, j)),
                  pl.BlockSpec((tf, d), lambda i, j: (j, 0)),
                  pl.BlockSpec((1, d), fixed),
                  pl.BlockSpec((1, d), fixed)],
        out_specs=pl.BlockSpec((tm, d), row),
        compiler_params=_params(2),
        name="mlp_ln",
    )(xb, h, w1, w2, g.reshape(1, d), b.reshape(1, d))


def _hgrn_body(layer, n_heads, q_ref, f_ref, i_ref, g_ref, lbl_ref, ng_ref, o_ref,
               st_ref, qd_ref, kd_ref, v_ref, acc_ref):
    t = q_ref.shape[0]
    dk = HGRN_HEAD_DIM

    @pl.when(pl.program_id(1) == 0)
    def _():
        st_ref[...] = jnp.zeros_like(st_ref)

    lg = lbl_ref[...]
    e = jnp.exp(lg - jnp.max(lg, axis=0, keepdims=True))
    lb = jnp.sum(e[:layer + 1], axis=0, keepdims=True) / jnp.sum(e, axis=0, keepdims=True)

    f = f_ref[...]
    e = jnp.exp(-jnp.abs(f))
    log_sig = jnp.minimum(f, 0.0) - jnp.log(1.0 + e)
    a = jnp.log(lb)
    b = jnp.log1p(-lb) + log_sig
    log_f = jnp.maximum(a, b) + jnp.log(1.0 + jnp.exp(-jnp.abs(a - b)))
    k = (1.0 - lb) * (jnp.where(f >= 0.0, e, 1.0) / (1.0 + e))
    q = q_ref[...]
    q = q * jax.nn.sigmoid(q) * dk ** -0.5

    r = lax.broadcasted_iota(jnp.int32, (t, t), 0)
    c = lax.broadcasted_iota(jnp.int32, (t, t), 1)
    same_chunk = (r // CHUNK) == (c // CHUNK)
    causal = same_chunk & (c <= r)
    tri = jnp.where(causal, 1.0, 0.0).astype(BF16)
    ones = jnp.where(same_chunk, 1.0, 0.0).astype(BF16)
    bc = _dot3(tri, log_f)
    b_last = _dot3(ones, log_f)

    qd_ref[...] = (q * jnp.exp(bc)).astype(BF16)
    k_inv = (k * jnp.exp(-bc)).astype(BF16)
    kd_ref[...] = (k * jnp.exp(b_last - bc)).astype(BF16)
    v_ref[...] = i_ref[...].astype(BF16)

    chunk_onehot = jnp.where(
        lax.broadcasted_iota(jnp.int32, (t, LANES), 0) // CHUNK
        == lax.broadcasted_iota(jnp.int32, (t, LANES), 1), 1.0, 0.0).astype(BF16)
    contract_rows = (((0,), (0,)), ((), ()))
    decay_cols = jnp.exp(sum(
        lax.dot_general(part, chunk_onehot, contract_rows, preferred_element_type=F32)
        for part in reversed(_split3(log_f))))

    for h in range(n_heads):
        cols = slice(h * dk, (h + 1) * dk)
        sc = lax.dot_general(qd_ref[:, cols], k_inv[:, cols], (((1,), (1,)), ((), ())),
                             preferred_element_type=F32)
        sc = jnp.where(causal, sc, 0.0).astype(BF16)
        acc_ref[:, cols] = jnp.dot(sc, v_ref[:, cols], preferred_element_type=F32)

    n_chunks = t // CHUNK
    chunk_of_row = lax.broadcasted_iota(jnp.int32, (t, dk), 0) // CHUNK
    for h in range(n_heads):
        cols = slice(h * dk, (h + 1) * dk)
        v = i_ref[:, cols]
        v_by_chunk = jnp.concatenate(
            [jnp.where(chunk_of_row == ci, v, 0.0) for ci in range(n_chunks)], axis=1)
        upd = lax.dot_general(kd_ref[:, cols], v_by_chunk.astype(BF16), contract_rows,
                              preferred_element_type=F32)
        st = st_ref[h]
        for ci in range(n_chunks):
            rows = slice(ci * CHUNK, (ci + 1) * CHUNK)
            acc_ref[rows, cols] += jnp.dot(qd_ref[rows, cols], st.astype(BF16),
                                           preferred_element_type=F32)
            st = decay_cols[h * dk:(h + 1) * dk, ci:ci + 1] * st + upd[:, ci * dk:(ci + 1) * dk]
        st_ref[h] = st

    ng = ng_ref[...]
    g = g_ref[...]
    gate = g * jax.nn.sigmoid(g)
    for h in range(n_heads):
        cols = slice(h * dk, (h + 1) * dk)
        o = acc_ref[:, cols]
        o = o * lax.rsqrt(jnp.mean(o * o, axis=-1, keepdims=True) + RMS_EPS) * ng
        o_ref[:, cols] = (o * gate[:, cols]).astype(BF16)


def _hgrn(proj, lb_logits, norm_g, layer, t_block, heads_per_step):
    s, d4 = proj.shape
    d = d4 // 4
    t_block = min(t_block, s)
    gw = heads_per_step * HGRN_HEAD_DIM
    n_groups = d // gw
    n_lb = lb_logits.shape[0]
    col = lambda off: (lambda hg, tb: (tb, off * n_groups + hg))
    return pl.pallas_call(
        functools.partial(_hgrn_body, layer, heads_per_step),
        out_shape=jax.ShapeDtypeStruct((s, d), BF16),
        grid=(n_groups, s // t_block),
        in_specs=[pl.BlockSpec((t_block, gw), col(0)),
                  pl.BlockSpec((t_block, gw), col(1)),
                  pl.BlockSpec((t_block, gw), col(2)),
                  pl.BlockSpec((t_block, gw), col(3)),
                  pl.BlockSpec((n_lb, gw), lambda hg, tb: (0, hg)),
                  pl.BlockSpec((1, HGRN_HEAD_DIM), lambda hg, tb: (0, 0))],
        out_specs=pl.BlockSpec((t_block, gw), lambda hg, tb: (tb, hg)),
        scratch_shapes=[pltpu.VMEM((heads_per_step, HGRN_HEAD_DIM, HGRN_HEAD_DIM), F32),
                        pltpu.VMEM((t_block, gw), BF16),
                        pltpu.VMEM((t_block, gw), BF16),
                        pltpu.VMEM((t_block, gw), BF16),
                        pltpu.VMEM((t_block, gw), F32)],
        compiler_params=_params(2),
        name="hgrn2",
    )(proj, proj, proj, proj, lb_logits, norm_g.reshape(1, HGRN_HEAD_DIM))


def _max_sq_row_norm(ref, chunk):
    def body(c, best):
        x = ref[pl.ds(pl.multiple_of(c * chunk, chunk), chunk), :].astype(F32)
        sq = jnp.sum(x * x, axis=-1, keepdims=True)
        return jnp.maximum(best, jnp.max(sq, axis=0, keepdims=True))
    return lax.fori_loop(0, ref.shape[0] // chunk, body, jnp.zeros((1, 1), F32))


def _stickbreak_body(q_ref, k_ref, v_ref, o_ref, acc_ref, lw_ref, knorm_ref):
    tq, d = q_ref.shape
    tk = lw_ref.shape[1]
    cw = min(SB_CUMSUM_WIDTH, tk)
    first_row = pl.program_id(1) * tq
    diag = first_row // tk
    q = q_ref[...] * d ** -0.5
    r = lax.broadcasted_iota(jnp.int32, (cw, cw), 0)
    c = lax.broadcasted_iota(jnp.int32, (cw, cw), 1)
    incl_rev = jnp.where(r >= c, 1.0, 0.0).astype(BF16)

    def log_weights(j, diagonal):
        rows = pl.ds(pl.multiple_of(j * tk, tk), tk)
        z = lax.dot_general(q, k_ref[rows, :], (((1,), (1,)), ((), ())),
                            preferred_element_type=F32)
        if diagonal:
            ahead = (lax.broadcasted_iota(jnp.int32, (tq, tk), 1)
                     - lax.broadcasted_iota(jnp.int32, (tq, tk), 0))
            z = jnp.where(ahead < first_row - j * tk, z, MASK_VALUE)
        sp = jnp.maximum(z, 0.0) + jnp.log(1.0 + jnp.exp2(jnp.abs(z) * -LOG2E))
        sp = sp.astype(BF16)
        later = None
        for sub in reversed(range(tk // cw)):
            cols = slice(sub * cw, (sub + 1) * cw)
            within = jnp.dot(sp[:, cols], incl_rev, preferred_element_type=F32)
            total = within if later is None else within + later
            lw_ref[:, cols] = z[:, cols] - total
            later = total[:, 0:1]
        return later

    def accumulate(j, offs):
        rows = pl.ds(pl.multiple_of(j * tk, tk), tk)
        w = jnp.exp(lw_ref[...] - offs)
        acc_ref[...] += jnp.dot(w.astype(BF16), v_ref[rows, :], preferred_element_type=F32)

    @pl.when(pl.program_id(1) == 0)
    def _():
        knorm_ref[...] = jnp.broadcast_to(_max_sq_row_norm(k_ref, tk), knorm_ref.shape)

    qf = q.astype(F32)
    q_sq = jnp.max(jnp.sum(qf * qf, axis=-1, keepdims=True), axis=0, keepdims=True)
    slack = jnp.sqrt(q_sq * knorm_ref[0:1, 0:1]) * 2.0 ** -8

    def earlier_blocks_matter(offs):
        return (jnp.min(offs - slack) <= UNDERFLOW_MARGIN).astype(jnp.int32)

    def wanted(carry):
        n, _, _, more = carry
        return jnp.logical_and(n < diag, more == 1)

    def step(carry):
        n, offs, block_sum, _ = carry
        j = diag - n
        accumulate(j, offs)
        offs = offs + block_sum
        block_sum = log_weights(j - 1, False)
        return n + 1, offs, block_sum, earlier_blocks_matter(offs + block_sum)

    acc_ref[...] = jnp.zeros_like(acc_ref)
    block_sum = log_weights(diag, True)
    carry = (jnp.int32(0), jnp.zeros((tq, 1), F32), block_sum, earlier_blocks_matter(block_sum))
    n, offs, _, _ = lax.while_loop(wanted, step, carry)
    accumulate(diag - n, offs)
    o_ref[...] = acc_ref[...].astype(o_ref.dtype)


def _stickbreak(qkv, n_heads, tq, tk):
    s = qkv.shape[0]
    d = ATT_HEAD_DIM
    tq, tk = min(tq, s), min(tk, s)
    assert s % tk == 0 and tk % tq == 0, (s, tq, tk)
    return pl.pallas_call(
        _stickbreak_body,
        out_shape=jax.ShapeDtypeStruct((s, n_heads * d), BF16),
        grid=(n_heads, s // tq),
        in_specs=[pl.BlockSpec((tq, d), lambda h, i: (i, h)),
                  pl.BlockSpec((s, d), lambda h, i: (0, n_heads + h)),
                  pl.BlockSpec((s, d), lambda h, i: (0, 2 * n_heads + h))],
        out_specs=pl.BlockSpec((tq, d), lambda h, i: (i, h)),
        scratch_shapes=[pltpu.VMEM((tq, d), F32),
                        pltpu.VMEM((tq, tk), F32),
                        pltpu.VMEM((SUBLANES, LANES), F32)],
        compiler_params=_params(2),
        name="stickbreak_attn",
    )(qkv, qkv, qkv)


def _fox_gate_body(fg_ref, bf_ref, cum_ref, carry_ref):
    t = fg_ref.shape[0]

    @pl.when(pl.program_id(0) == 0)
    def _():
        carry_ref[...] = jnp.zeros_like(carry_ref)

    x = fg_ref[...] + bf_ref[...]
    log_f = jnp.minimum(x, 0.0) - jnp.log1p(jnp.exp(-jnp.abs(x)))
    r = lax.broadcasted_iota(jnp.int32, (t, t), 0)
    c = lax.broadcasted_iota(jnp.int32, (t, t), 1)
    tri = jnp.where(c <= r, 1.0, 0.0).astype(BF16)
    cum = _dot3(tri, log_f) + carry_ref[...]
    cum_ref[...] = cum
    carry_ref[...] = cum[t - 1:t, :]


def _fox_gates(fg, bf, t_block):
    s, w = fg.shape
    t_block = min(t_block, s)
    return pl.pallas_call(
        _fox_gate_body,
        out_shape=jax.ShapeDtypeStruct((s, w), F32),
        grid=(s // t_block,),
        in_specs=[pl.BlockSpec((t_block, w), lambda i: (i, 0)),
                  pl.BlockSpec((1, w), lambda i: (0, 0))],
        out_specs=pl.BlockSpec((t_block, w), lambda i: (i, 0)),
        scratch_shapes=[pltpu.VMEM((1, w), F32)],
        compiler_params=_params(1),
        name="fox_gates",
    )(fg, bf)


def _fox_body(q_ref, k_ref, v_ref, cq_ref, ck_ref, o_ref, acc_ref, z_ref, knorm_ref):
    tq, d = q_ref.shape
    tk = z_ref.shape[1]
    first_row = pl.program_id(1) * tq
    diag = first_row // tk
    q = q_ref[...] * d ** -0.5
    cq = cq_ref[...]

    def logits(j, diagonal):
        rows = pl.ds(pl.multiple_of(j * tk, tk), tk)
        z = lax.dot_general(q, k_ref[rows, :], (((1,), (1,)), ((), ())),
                            preferred_element_type=F32)
        z = (z + cq) - ck_ref[:, rows]
        if diagonal:
            ahead = (lax.broadcasted_iota(jnp.int32, (tq, tk), 1)
                     - lax.broadcasted_iota(jnp.int32, (tq, tk), 0))
            z = jnp.where(ahead <= first_row - j * tk, z, MASK_VALUE)
        z_ref[...] = z
        return jnp.max(z, axis=-1, keepdims=True)

    def accumulate(j, m, l, row_max):
        rows = pl.ds(pl.multiple_of(j * tk, tk), tk)
        m_new = jnp.maximum(m, row_max)
        scale = jnp.exp(m - m_new)
        p = jnp.exp(z_ref[...] - m_new)
        l = scale * l + jnp.sum(p, axis=-1, keepdims=True)
        acc_ref[...] = scale * acc_ref[...] + jnp.dot(
            p.astype(BF16), v_ref[rows, :], preferred_element_type=F32)
        return m_new, l

    @pl.when(pl.program_id(1) == 0)
    def _():
        knorm_ref[...] = jnp.broadcast_to(_max_sq_row_norm(k_ref, tk), knorm_ref.shape)

    qf = q.astype(F32)
    q_sq = jnp.sum(qf * qf, axis=-1, keepdims=True)
    reach = jnp.sqrt(q_sq * knorm_ref[0:1, 0:1]) + cq

    def blocks_up_to_matter(last_block, m):
        last_block = jnp.maximum(last_block, 0)
        ck_end = jnp.min(ck_ref[:, pl.ds(pl.multiple_of((last_block + 1) * tk - LANES, LANES),
                                         LANES)])
        return (jnp.min(m - reach) + ck_end <= UNDERFLOW_MARGIN).astype(jnp.int32)

    def wanted(carry):
        n, _, _, _, more = carry
        return jnp.logical_and(n < diag, more == 1)

    def step(carry):
        n, m, l, row_max, _ = carry
        j = diag - n
        m, l = accumulate(j, m, l, row_max)
        row_max = logits(j - 1, False)
        return n + 1, m, l, row_max, blocks_up_to_matter(j - 2, jnp.maximum(m, row_max))

    acc_ref[...] = jnp.zeros_like(acc_ref)
    row_max = logits(diag, True)
    carry = (jnp.int32(0), jnp.full((tq, 1), MASK_VALUE, F32), jnp.zeros((tq, 1), F32),
             row_max, blocks_up_to_matter(diag - 1, row_max))
    n, m, l, row_max, _ = lax.while_loop(wanted, step, carry)
    _, l = accumulate(diag - n, m, l, row_max)
    o_ref[...] = (acc_ref[...] / l).astype(o_ref.dtype)


def _fox(qkv, cum_col, cum_row, n_heads, tq, tk):
    s = qkv.shape[0]
    d = ATT_HEAD_DIM
    tq, tk = min(tq, s), min(tk, s)
    assert s % tk == 0 and tk % tq == 0, (s, tq, tk)
    return pl.pallas_call(
        _fox_body,
        out_shape=jax.ShapeDtypeStruct((s, n_heads * d), BF16),
        grid=(n_heads, s // tq),
        in_specs=[pl.BlockSpec((tq, d), lambda h, i: (i, h)),
                  pl.BlockSpec((s, d), lambda h, i: (0, n_heads + h)),
                  pl.BlockSpec((s, d), lambda h, i: (0, 2 * n_heads + h)),
                  pl.BlockSpec((None, tq, 1), lambda h, i: (h, i, 0)),
                  pl.BlockSpec((None, 1, s), lambda h, i: (h, 0, 0))],
        out_specs=pl.BlockSpec((tq, d), lambda h, i: (i, h)),
        scratch_shapes=[pltpu.VMEM((tq, d), F32),
                        pltpu.VMEM((tq, tk), F32),
                        pltpu.VMEM((SUBLANES, LANES), F32)],
        compiler_params=_params(2),
        name="fox_attn",
    )(qkv, qkv, qkv, cum_col, cum_row)


def _conv_outproj_ln_body(bg_ref, cg_ref, hd_ref, cgp_ref, hdp_ref, cw_ref, w_ref, h_ref,
                          g_ref, b_ref, of_ref, ob_ref):
    i = pl.program_id(0)
    u = cg_ref[...] * hd_ref[...]
    u_prev = jnp.where(i > 0, cgp_ref[...] * hdp_ref[...], 0.0)
    row8 = lax.broadcasted_iota(jnp.int32, u_prev.shape, 0)
    y = cw_ref[CONV_WIDTH - 1:CONV_WIDTH, :] * u
    for lag in range(1, CONV_WIDTH):
        shifted = pltpu.roll(u, lag, 0)
        head = jnp.where(row8 < lag, pltpu.roll(u_prev, lag, 0), shifted[:SUBLANES])
        shifted = jnp.concatenate([head, shifted[SUBLANES:]], axis=0)
        y += cw_ref[CONV_WIDTH - 1 - lag:CONV_WIDTH - lag, :] * shifted
    acc = jnp.dot((bg_ref[...] * y).astype(BF16), w_ref[...], preferred_element_type=F32)
    out = _layer_norm(ALPHA * h_ref[...] + acc, g_ref[...], b_ref[...])
    of_ref[...] = out
    ob_ref[...] = out.astype(BF16)


def _conv_outproj_ln(proj, conv_w, w, h, g, b, tm):
    m, d = h.shape
    tm = min(tm, m)
    blocks_per_tile = tm // SUBLANES
    col = lambda off: (lambda i: (i, off))
    prev = lambda off: (lambda i: (jnp.maximum(i * blocks_per_tile - 1, 0), off))
    row = lambda i: (i, 0)
    fixed = lambda i: (0, 0)
    return pl.pallas_call(
        _conv_outproj_ln_body,
        out_shape=(jax.ShapeDtypeStruct((m, d), F32),
                   jax.ShapeDtypeStruct((m, d), BF16)),
        grid=(m // tm,),
        in_specs=[pl.BlockSpec((tm, d), col(0)),
                  pl.BlockSpec((tm, d), col(1)),
                  pl.BlockSpec((tm, d), col(2)),
                  pl.BlockSpec((SUBLANES, d), prev(1)),
                  pl.BlockSpec((SUBLANES, d), prev(2)),
                  pl.BlockSpec((CONV_WIDTH, d), fixed),
                  pl.BlockSpec((d, d), fixed),
                  pl.BlockSpec((tm, d), row),
                  pl.BlockSpec((1, d), fixed),
                  pl.BlockSpec((1, d), fixed)],
        out_specs=(pl.BlockSpec((tm, d), row), pl.BlockSpec((tm, d), row)),
        compiler_params=_params(1),
        name="conv_outproj_ln",
    )(proj, proj, proj, proj, proj, conv_w, w, h, g.reshape(1, d), b.reshape(1, d))


def _tiles(s):
    return dict(
        proj_tm=min(1024, s), proj_tn=1024,
        out_tm=min(512, s), conv_tm=min(256, s),
        mlp_tm=min(1024, s), mlp_tf=512,
        hgrn_t=min(256, s), hgrn_heads=8,
        sb_tq=min(512, s), sb_tk=min(512, s), fox_tq=min(512, s), fox_tk=min(512, s),
        gate_t=min(256, s),
    )


def kernel(x, w_mix_a, norm_g_a, lb_logits, w_out_a, w_mix_b, w_out_b, w_mix_c, b_f_c,
           w_out_c, w_mix_d, conv_w_d, w_out_d, ln_mix_g, ln_mix_b, w_ff1, w_ff2,
           ln_ff_g, ln_ff_b):
    bsz, s, d = x.shape
    n_att_heads = d // ATT_HEAD_DIM
    tl = _tiles(s)
    outs = []
    for bi in range(bsz):
        h = x.reshape(s, d) if bsz == 1 else x[bi]
        for i in range(DEPTH):
            m, j = i % N_MIXERS, i // N_MIXERS
            g, b = ln_mix_g[i], ln_mix_b[i]
            if m == 0:
                proj = _proj(h, w_mix_a[j].astype(BF16), F32, tl["proj_tm"], tl["proj_tn"])
                y = _hgrn(proj, lb_logits, norm_g_a[j], i, tl["hgrn_t"], tl["hgrn_heads"])
                h, hb = _outproj_ln(y, w_out_a[j].astype(BF16), h, g, b, tl["out_tm"])
            elif m == 1:
                qkv = _proj(h, w_mix_b[j].astype(BF16), BF16, tl["proj_tm"], tl["proj_tn"])
                y = _stickbreak(qkv, n_att_heads, tl["sb_tq"], tl["sb_tk"])
                h, hb = _outproj_ln(y, w_out_b[j].astype(BF16), h, g, b, tl["out_tm"])
            elif m == 2:
                wc = w_mix_c[j].astype(BF16)
                qkv = _proj(h, wc, BF16, tl["proj_tm"], tl["proj_tn"], n=3 * d)
                lane_pad = HGRN_HEAD_DIM - n_att_heads
                wf = jnp.pad(wc[:, 3 * d:], ((0, 0), (0, lane_pad)))
                fg = _proj(h, wf, F32, tl["proj_tm"], tl["proj_tn"])
                bf = jnp.pad(b_f_c[j], (0, lane_pad)).reshape(1, -1)
                cum = _fox_gates(fg, bf, tl["gate_t"])[:, :n_att_heads]
                cum_t = cum.T
                y = _fox(qkv, cum_t[:, :, None], cum_t[:, None, :], n_att_heads,
                         tl["fox_tq"], tl["fox_tk"])
                h, hb = _outproj_ln(y, w_out_c[j].astype(BF16), h, g, b, tl["out_tm"])
            else:
                proj = _proj(h, w_mix_d[j].astype(BF16), F32, tl["proj_tm"], tl["proj_tn"])
                h, hb = _conv_outproj_ln(proj, conv_w_d[j], w_out_d[j].astype(BF16), h, g, b,
                                         tl["conv_tm"])
            h = _mlp_ln(hb, h, w_ff1[i].astype(BF16), w_ff2[i].astype(BF16),
                        ln_ff_g[i], ln_ff_b[i], tl["mlp_tm"], tl["mlp_tf"])
        outs.append(h)
    return outs[0][None] if bsz == 1 else jnp.stack(outs, axis=0)
```

```python
import functools

import jax
import jax.numpy as jnp
from jax import lax
from jax.experimental import pallas as pl
from jax.experimental.pallas import tpu as pltpu

DEPTH = 4
N_MIXERS = 4
HGRN_HEAD_DIM = 128
ATT_HEAD_DIM = 256
CHUNK = 16
CONV_WIDTH = 3
LN_EPS = 1e-5
RMS_EPS = 1e-6
ALPHA = (2.0 * DEPTH) ** 0.25
MASK_VALUE = -1e30
UNDERFLOW_MARGIN = 100.0
LOG2E = 1.4426950408889634
V7X_MXU_WIDTH = 256
OUTPROJ_ROW_STRIP = 128
SB_CUMSUM_WIDTH = V7X_MXU_WIDTH

V7X_VMEM_LIMIT_BYTES = 56 * 1024 * 1024
SUBLANES = 8
LANES = 128

F32 = jnp.float32
BF16 = jnp.bfloat16


def _params(n_grid_axes, vmem_bytes=V7X_VMEM_LIMIT_BYTES):
    return pltpu.CompilerParams(
        dimension_semantics=("arbitrary",) * n_grid_axes,
        vmem_limit_bytes=vmem_bytes)


def _layer_norm(v, g, b):
    mu = jnp.mean(v, axis=-1, keepdims=True)
    c = v - mu
    var = jnp.mean(c * c, axis=-1, keepdims=True)
    return c * lax.rsqrt(var + LN_EPS) * g + b


def _split3(x):
    hi = x.astype(BF16)
    r1 = x - hi.astype(F32)
    mid = r1.astype(BF16)
    lo = (r1 - mid.astype(F32)).astype(BF16)
    return hi, mid, lo


def _dot3(mat01, x):
    hi, mid, lo = _split3(x)
    out = jnp.dot(mat01, lo, preferred_element_type=F32)
    out += jnp.dot(mat01, mid, preferred_element_type=F32)
    out += jnp.dot(mat01, hi, preferred_element_type=F32)
    return out


def _softplus(z):
    return jnp.maximum(z, 0.0) + jnp.log1p(jnp.exp(-jnp.abs(z)))


def _proj_body(x_ref, w_ref, o_ref, xb_ref):
    @pl.when(pl.program_id(1) == 0)
    def _():
        xb_ref[...] = x_ref[...].astype(BF16)

    o_ref[...] = jnp.dot(xb_ref[...], w_ref[...],
                         preferred_element_type=F32).astype(o_ref.dtype)


def _proj(x, w, out_dtype, tm, tn, n=None):
    m, k = x.shape
    n = w.shape[1] if n is None else n
    tm, tn = min(tm, m), min(tn, n)
    assert m % tm == 0 and n % tn == 0, (m, n, tm, tn)
    return pl.pallas_call(
        _proj_body,
        out_shape=jax.ShapeDtypeStruct((m, n), out_dtype),
        grid=(m // tm, n // tn),
        in_specs=[pl.BlockSpec((tm, k), lambda i, j: (i, 0)),
                  pl.BlockSpec((k, tn), lambda i, j: (0, j))],
        out_specs=pl.BlockSpec((tm, tn), lambda i, j: (i, j)),
        scratch_shapes=[pltpu.VMEM((tm, k), BF16)],
        compiler_params=_params(2),
        name="proj",
    )(x, w)


def _outproj_ln_body(y_ref, w_ref, h_ref, g_ref, b_ref, of_ref, ob_ref):
    tm = y_ref.shape[0]
    strip = min(OUTPROJ_ROW_STRIP, tm)
    for r0 in range(0, tm, strip):
        rows = slice(r0, r0 + strip)
        acc = jnp.dot(y_ref[rows, :], w_ref[...], preferred_element_type=F32)
        out = _layer_norm(ALPHA * h_ref[rows, :] + acc, g_ref[...], b_ref[...])
        of_ref[rows, :] = out
        ob_ref[rows, :] = out.astype(BF16)


def _outproj_ln(y, w, h, g, b, tm):
    m, d = h.shape
    tm = min(tm, m)
    row = lambda i: (i, 0)
    fixed = lambda i: (0, 0)
    return pl.pallas_call(
        _outproj_ln_body,
        out_shape=(jax.ShapeDtypeStruct((m, d), F32),
                   jax.ShapeDtypeStruct((m, d), BF16)),
        grid=(m // tm,),
        in_specs=[pl.BlockSpec((tm, d), row),
                  pl.BlockSpec((d, d), fixed),
                  pl.BlockSpec((tm, d), row),
                  pl.BlockSpec((1, d), fixed),
                  pl.BlockSpec((1, d), fixed)],
        out_specs=(pl.BlockSpec((tm, d), row), pl.BlockSpec((tm, d), row)),
        compiler_params=_params(1),
        name="outproj_ln",
    )(y, w, h, g.reshape(1, d), b.reshape(1, d))


def _mlp_ln_body(xb_ref, h_ref, w1_ref, w2_ref, g_ref, b_ref, of_ref):
    j = pl.program_id(1)

    @pl.when(j == 0)
    def _():
        of_ref[...] = ALPHA * h_ref[...]

    a = jnp.dot(xb_ref[...], w1_ref[...], preferred_element_type=F32)
    a = jnp.maximum(a, 0.0)
    of_ref[...] += jnp.dot((a * a).astype(BF16), w2_ref[...], preferred_element_type=F32)

    @pl.when(j == pl.num_programs(1) - 1)
    def _():
        of_ref[...] = _layer_norm(of_ref[...], g_ref[...], b_ref[...])


def _mlp_ln(xb, h, w1, w2, layer, g, b, tm, tf):
    m, d = h.shape
    f = w1.shape[2]
    tm, tf = min(tm, m), min(tf, f)
    row = lambda i, j: (i, 0)
    fixed = lambda i, j: (0, 0)
    return pl.pallas_call(
        _mlp_ln_body,
        out_shape=jax.ShapeDtypeStruct((m, d), F32),
        grid=(m // tm, f // tf),
        in_specs=[pl.BlockSpec((tm, d), row),
                  pl.BlockSpec((tm, d), row),
                  pl.BlockSpec((None, d, tf), lambda i, j: (layer, 0, j)),
                  pl.BlockSpec((None, tf, d), lambda i, j: (layer, j, 0)),
                  pl.BlockSpec((1, d), fixed),
                  pl.BlockSpec((1, d), fixed)],
        out_specs=pl.BlockSpec((tm, d), row),
        compiler_params=_params(2),
        name="mlp_ln",
    )(xb, h, w1, w2, g.reshape(1, d), b.reshape(1, d))


def _hgrn_body(layer, n_heads, q_ref, f_ref, i_ref, g_ref, lbl_ref, ng_ref, o_ref,
               st_ref, qd_ref, kd_ref, v_ref, acc_ref):
    t = q_ref.shape[0]
    dk = HGRN_HEAD_DIM

    @pl.when(pl.program_id(1) == 0)
    def _():
        st_ref[...] = jnp.zeros_like(st_ref)

    lg = lbl_ref[...]
    e = jnp.exp(lg - jnp.max(lg, axis=0, keepdims=True))
    lb = jnp.sum(e[:layer + 1], axis=0, keepdims=True) / jnp.sum(e, axis=0, keepdims=True)

    f = f_ref[...]
    e = jnp.exp(-jnp.abs(f))
    log_sig = jnp.minimum(f, 0.0) - jnp.log(1.0 + e)
    a = jnp.log(lb)
    b = jnp.log1p(-lb) + log_sig
    log_f = jnp.maximum(a, b) + jnp.log(1.0 + jnp.exp(-jnp.abs(a - b)))
    k = (1.0 - lb) * (jnp.where(f >= 0.0, e, 1.0) / (1.0 + e))
    q = q_ref[...]
    q = q * jax.nn.sigmoid(q) * dk ** -0.5

    r = lax.broadcasted_iota(jnp.int32, (t, t), 0)
    c = lax.broadcasted_iota(jnp.int32, (t, t), 1)
    same_chunk = (r // CHUNK) == (c // CHUNK)
    causal = same_chunk & (c <= r)
    tri = jnp.where(causal, 1.0, 0.0).astype(BF16)
    ones = jnp.where(same_chunk, 1.0, 0.0).astype(BF16)
    bc = _dot3(tri, log_f)
    b_last = _dot3(ones, log_f)

    qd_ref[...] = (q * jnp.exp(bc)).astype(BF16)
    k_inv = (k * jnp.exp(-bc)).astype(BF16)
    kd_ref[...] = (k * jnp.exp(b_last - bc)).astype(BF16)
    v_ref[...] = i_ref[...].astype(BF16)

    chunk_onehot = jnp.where(
        lax.broadcasted_iota(jnp.int32, (t, LANES), 0) // CHUNK
        == lax.broadcasted_iota(jnp.int32, (t, LANES), 1), 1.0, 0.0).astype(BF16)
    contract_rows = (((0,), (0,)), ((), ()))
    decay_cols = jnp.exp(sum(
        lax.dot_general(part, chunk_onehot, contract_rows, preferred_element_type=F32)
        for part in reversed(_split3(log_f))))

    for h in range(n_heads):
        cols = slice(h * dk, (h + 1) * dk)
        sc = lax.dot_general(qd_ref[:, cols], k_inv[:, cols], (((1,), (1,)), ((), ())),
                             preferred_element_type=F32)
        sc = jnp.where(causal, sc, 0.0).astype(BF16)
        acc_ref[:, cols] = jnp.dot(sc, v_ref[:, cols], preferred_element_type=F32)

    n_chunks = t // CHUNK
    chunk_of_row = lax.broadcasted_iota(jnp.int32, (t, dk), 0) // CHUNK
    for h in range(n_heads):
        cols = slice(h * dk, (h + 1) * dk)
        v = i_ref[:, cols]
        v_by_chunk = jnp.concatenate(
            [jnp.where(chunk_of_row == ci, v, 0.0) for ci in range(n_chunks)], axis=1)
        upd = lax.dot_general(kd_ref[:, cols], v_by_chunk.astype(BF16), contract_rows,
                              preferred_element_type=F32)
        st = st_ref[h]
        for ci in range(n_chunks):
            rows = slice(ci * CHUNK, (ci + 1) * CHUNK)
            acc_ref[rows, cols] += jnp.dot(qd_ref[rows, cols], st.astype(BF16),
                                           preferred_element_type=F32)
            st = decay_cols[h * dk:(h + 1) * dk, ci:ci + 1] * st + upd[:, ci * dk:(ci + 1) * dk]
        st_ref[h] = st

    ng = ng_ref[...]
    g = g_ref[...]
    gate = g * jax.nn.sigmoid(g)
    for h in range(n_heads):
        cols = slice(h * dk, (h + 1) * dk)
        o = acc_ref[:, cols]
        o = o * lax.rsqrt(jnp.mean(o * o, axis=-1, keepdims=True) + RMS_EPS) * ng
        o_ref[:, cols] = (o * gate[:, cols]).astype(BF16)


def _hgrn(proj, lb_logits, norm_g, layer, t_block, heads_per_step):
    s, d4 = proj.shape
    d = d4 // 4
    t_block = min(t_block, s)
    gw = heads_per_step * HGRN_HEAD_DIM
    n_groups = d // gw
    n_lb = lb_logits.shape[0]
    col = lambda off: (lambda hg, tb: (tb, off * n_groups + hg))
    return pl.pallas_call(
        functools.partial(_hgrn_body, layer, heads_per_step),
        out_shape=jax.ShapeDtypeStruct((s, d), BF16),
        grid=(n_groups, s // t_block),
        in_specs=[pl.BlockSpec((t_block, gw), col(0)),
                  pl.BlockSpec((t_block, gw), col(1)),
                  pl.BlockSpec((t_block, gw), col(2)),
                  pl.BlockSpec((t_block, gw), col(3)),
                  pl.BlockSpec((n_lb, gw), lambda hg, tb: (0, hg)),
                  pl.BlockSpec((1, HGRN_HEAD_DIM), lambda hg, tb: (0, 0))],
        out_specs=pl.BlockSpec((t_block, gw), lambda hg, tb: (tb, hg)),
        scratch_shapes=[pltpu.VMEM((heads_per_step, HGRN_HEAD_DIM, HGRN_HEAD_DIM), F32),
                        pltpu.VMEM((t_block, gw), BF16),
                        pltpu.VMEM((t_block, gw), BF16),
                        pltpu.VMEM((t_block, gw), BF16),
                        pltpu.VMEM((t_block, gw), F32)],
        compiler_params=_params(2),
        name="hgrn2",
    )(proj, proj, proj, proj, lb_logits, norm_g.reshape(1, HGRN_HEAD_DIM))


def _max_sq_row_norm(ref, chunk):
    def body(c, best):
        x = ref[pl.ds(pl.multiple_of(c * chunk, chunk), chunk), :].astype(F32)
        sq = jnp.sum(x * x, axis=-1, keepdims=True)
        return jnp.maximum(best, jnp.max(sq, axis=0, keepdims=True))
    return lax.fori_loop(0, ref.shape[0] // chunk, body, jnp.zeros((1, 1), F32))


def _stickbreak_body(q_ref, k_ref, v_ref, o_ref, acc_ref, lw_ref, knorm_ref):
    tq, d = q_ref.shape
    tk = lw_ref.shape[1]
    cw = min(SB_CUMSUM_WIDTH, tk)
    first_row = pl.program_id(1) * tq
    diag = first_row // tk
    q = q_ref[...] * d ** -0.5
    r = lax.broadcasted_iota(jnp.int32, (cw, cw), 0)
    c = lax.broadcasted_iota(jnp.int32, (cw, cw), 1)
    incl_rev = jnp.where(r >= c, 1.0, 0.0).astype(BF16)

    def log_weights(j, diagonal):
        rows = pl.ds(pl.multiple_of(j * tk, tk), tk)
        z = lax.dot_general(q, k_ref[rows, :], (((1,), (1,)), ((), ())),
                            preferred_element_type=F32)
        if diagonal:
            ahead = (lax.broadcasted_iota(jnp.int32, (tq, tk), 1)
                     - lax.broadcasted_iota(jnp.int32, (tq, tk), 0))
            z = jnp.where(ahead < first_row - j * tk, z, MASK_VALUE)
        sp = jnp.maximum(z, 0.0) + jnp.log(1.0 + jnp.exp2(jnp.abs(z) * -LOG2E))
        sp = sp.astype(BF16)
        later = None
        for sub in reversed(range(tk // cw)):
            cols = slice(sub * cw, (sub + 1) * cw)
            within = jnp.dot(sp[:, cols], incl_rev, preferred_element_type=F32)
            total = within if later is None else within + later
            lw_ref[:, cols] = z[:, cols] - total
            later = total[:, 0:1]
        return later

    def accumulate(j, offs):
        rows = pl.ds(pl.multiple_of(j * tk, tk), tk)
        w = jnp.exp(lw_ref[...] - offs)
        acc_ref[...] += jnp.dot(w.astype(BF16), v_ref[rows, :], preferred_element_type=F32)

    @pl.when(pl.program_id(1) == 0)
    def _():
        knorm_ref[...] = jnp.broadcast_to(_max_sq_row_norm(k_ref, tk), knorm_ref.shape)

    qf = q.astype(F32)
    q_sq = jnp.max(jnp.sum(qf * qf, axis=-1, keepdims=True), axis=0, keepdims=True)
    slack = jnp.sqrt(q_sq * knorm_ref[0:1, 0:1]) * 2.0 ** -8

    def earlier_blocks_matter(offs):
        return (jnp.min(offs - slack) <= UNDERFLOW_MARGIN).astype(jnp.int32)

    def wanted(carry):
        n, _, _, more = carry
        return jnp.logical_and(n < diag, more == 1)

    def step(carry):
        n, offs, block_sum, _ = carry
        j = diag - n
        accumulate(j, offs)
        offs = offs + block_sum
        block_sum = log_weights(j - 1, False)
        return n + 1, offs, block_sum, earlier_blocks_matter(offs + block_sum)

    acc_ref[...] = jnp.zeros_like(acc_ref)
    block_sum = log_weights(diag, True)
    carry = (jnp.int32(0), jnp.zeros((tq, 1), F32), block_sum, earlier_blocks_matter(block_sum))
    n, offs, _, _ = lax.while_loop(wanted, step, carry)
    accumulate(diag - n, offs)
    o_ref[...] = acc_ref[...].astype(o_ref.dtype)


def _stickbreak(qkv, n_heads, tq, tk):
    s = qkv.shape[0]
    d = ATT_HEAD_DIM
    tq, tk = min(tq, s), min(tk, s)
    assert s % tk == 0 and tk % tq == 0, (s, tq, tk)
    return pl.pallas_call(
        _stickbreak_body,
        out_shape=jax.ShapeDtypeStruct((s, n_heads * d), BF16),
        grid=(n_heads, s // tq),
        in_specs=[pl.BlockSpec((tq, d), lambda h, i: (i, h)),
                  pl.BlockSpec((s, d), lambda h, i: (0, n_heads + h)),
                  pl.BlockSpec((s, d), lambda h, i: (0, 2 * n_heads + h))],
        out_specs=pl.BlockSpec((tq, d), lambda h, i: (i, h)),
        scratch_shapes=[pltpu.VMEM((tq, d), F32),
                        pltpu.VMEM((tq, tk), F32),
                        pltpu.VMEM((SUBLANES, LANES), F32)],
        compiler_params=_params(2),
        name="stickbreak_attn",
    )(qkv, qkv, qkv)


def _fox_gate_body(fg_ref, bf_ref, cum_ref, carry_ref):
    t = fg_ref.shape[0]

    @pl.when(pl.program_id(0) == 0)
    def _():
        carry_ref[...] = jnp.zeros_like(carry_ref)

    x = fg_ref[...] + bf_ref[...]
    log_f = jnp.minimum(x, 0.0) - jnp.log1p(jnp.exp(-jnp.abs(x)))
    r = lax.broadcasted_iota(jnp.int32, (t, t), 0)
    c = lax.broadcasted_iota(jnp.int32, (t, t), 1)
    tri = jnp.where(c <= r, 1.0, 0.0).astype(BF16)
    cum = _dot3(tri, log_f) + carry_ref[...]
    cum_ref[...] = cum
    carry_ref[...] = cum[t - 1:t, :]


def _fox_gates(fg, bf, t_block):
    s, w = fg.shape
    t_block = min(t_block, s)
    return pl.pallas_call(
        _fox_gate_body,
        out_shape=jax.ShapeDtypeStruct((s, w), F32),
        grid=(s // t_block,),
        in_specs=[pl.BlockSpec((t_block, w), lambda i: (i, 0)),
                  pl.BlockSpec((1, w), lambda i: (0, 0))],
        out_specs=pl.BlockSpec((t_block, w), lambda i: (i, 0)),
        scratch_shapes=[pltpu.VMEM((1, w), F32)],
        compiler_params=_params(1),
        name="fox_gates",
    )(fg, bf)


def _fox_body(q_ref, k_ref, v_ref, cq_ref, ck_ref, o_ref, acc_ref, z_ref, knorm_ref):
    tq, d = q_ref.shape
    tk = z_ref.shape[1]
    first_row = pl.program_id(1) * tq
    diag = first_row // tk
    q = q_ref[...] * d ** -0.5
    cq = cq_ref[...]

    def logits(j, diagonal):
        rows = pl.ds(pl.multiple_of(j * tk, tk), tk)
        z = lax.dot_general(q, k_ref[rows, :], (((1,), (1,)), ((), ())),
                            preferred_element_type=F32)
        z = (z + cq) - ck_ref[:, rows]
        if diagonal:
            ahead = (lax.broadcasted_iota(jnp.int32, (tq, tk), 1)
                     - lax.broadcasted_iota(jnp.int32, (tq, tk), 0))
            z = jnp.where(ahead <= first_row - j * tk, z, MASK_VALUE)
        z_ref[...] = z
        return jnp.max(z, axis=-1, keepdims=True)

    def accumulate(j, m, l, row_max):
        rows = pl.ds(pl.multiple_of(j * tk, tk), tk)
        m_new = jnp.maximum(m, row_max)
        scale = jnp.exp(m - m_new)
        p = jnp.exp(z_ref[...] - m_new)
        l = scale * l + jnp.sum(p, axis=-1, keepdims=True)
        acc_ref[...] = scale * acc_ref[...] + jnp.dot(
            p.astype(BF16), v_ref[rows, :], preferred_element_type=F32)
        return m_new, l

    @pl.when(pl.program_id(1) == 0)
    def _():
        knorm_ref[...] = jnp.broadcast_to(_max_sq_row_norm(k_ref, tk), knorm_ref.shape)

    qf = q.astype(F32)
    q_sq = jnp.sum(qf * qf, axis=-1, keepdims=True)
    reach = jnp.sqrt(q_sq * knorm_ref[0:1, 0:1]) + cq

    def blocks_up_to_matter(last_block, m):
        last_block = jnp.maximum(last_block, 0)
        ck_end = jnp.min(ck_ref[:, pl.ds(pl.multiple_of((last_block + 1) * tk - LANES, LANES),
                                         LANES)])
        return (jnp.min(m - reach) + ck_end <= UNDERFLOW_MARGIN).astype(jnp.int32)

    def wanted(carry):
        n, _, _, _, more = carry
        return jnp.logical_and(n < diag, more == 1)

    def step(carry):
        n, m, l, row_max, _ = carry
        j = diag - n
        m, l = accumulate(j, m, l, row_max)
        row_max = logits(j - 1, False)
        return n + 1, m, l, row_max, blocks_up_to_matter(j - 2, jnp.maximum(m, row_max))

    acc_ref[...] = jnp.zeros_like(acc_ref)
    row_max = logits(diag, True)
    carry = (jnp.int32(0), jnp.full((tq, 1), MASK_VALUE, F32), jnp.zeros((tq, 1), F32),
             row_max, blocks_up_to_matter(diag - 1, row_max))
    n, m, l, row_max, _ = lax.while_loop(wanted, step, carry)
    _, l = accumulate(diag - n, m, l, row_max)
    o_ref[...] = (acc_ref[...] / l).astype(o_ref.dtype)


def _fox(qkv, cum_col, cum_row, n_heads, tq, tk):
    s = qkv.shape[0]
    d = ATT_HEAD_DIM
    tq, tk = min(tq, s), min(tk, s)
    assert s % tk == 0 and tk % tq == 0, (s, tq, tk)
    return pl.pallas_call(
        _fox_body,
        out_shape=jax.ShapeDtypeStruct((s, n_heads * d), BF16),
        grid=(n_heads, s // tq),
        in_specs=[pl.BlockSpec((tq, d), lambda h, i: (i, h)),
                  pl.BlockSpec((s, d), lambda h, i: (0, n_heads + h)),
                  pl.BlockSpec((s, d), lambda h, i: (0, 2 * n_heads + h)),
                  pl.BlockSpec((None, tq, 1), lambda h, i: (h, i, 0)),
                  pl.BlockSpec((None, 1, s), lambda h, i: (h, 0, 0))],
        out_specs=pl.BlockSpec((tq, d), lambda h, i: (i, h)),
        scratch_shapes=[pltpu.VMEM((tq, d), F32),
                        pltpu.VMEM((tq, tk), F32),
                        pltpu.VMEM((SUBLANES, LANES), F32)],
        compiler_params=_params(2),
        name="fox_attn",
    )(qkv, qkv, qkv, cum_col, cum_row)


def _conv_outproj_ln_body(bg_ref, cg_ref, hd_ref, cgp_ref, hdp_ref, cw_ref, w_ref, h_ref,
                          g_ref, b_ref, of_ref, ob_ref):
    i = pl.program_id(0)
    u = cg_ref[...] * hd_ref[...]
    u_prev = jnp.where(i > 0, cgp_ref[...] * hdp_ref[...], 0.0)
    row8 = lax.broadcasted_iota(jnp.int32, u_prev.shape, 0)
    y = cw_ref[CONV_WIDTH - 1:CONV_WIDTH, :] * u
    for lag in range(1, CONV_WIDTH):
        shifted = pltpu.roll(u, lag, 0)
        head = jnp.where(row8 < lag, pltpu.roll(u_prev, lag, 0), shifted[:SUBLANES])
        shifted = jnp.concatenate([head, shifted[SUBLANES:]], axis=0)
        y += cw_ref[CONV_WIDTH - 1 - lag:CONV_WIDTH - lag, :] * shifted
    acc = jnp.dot((bg_ref[...] * y).astype(BF16), w_ref[...], preferred_element_type=F32)
    out = _layer_norm(ALPHA * h_ref[...] + acc, g_ref[...], b_ref[...])
    of_ref[...] = out
    ob_ref[...] = out.astype(BF16)


def _conv_outproj_ln(proj, conv_w, w, h, g, b, tm):
    m, d = h.shape
    tm = min(tm, m)
    blocks_per_tile = tm // SUBLANES
    col = lambda off: (lambda i: (i, off))
    prev = lambda off: (lambda i: (jnp.maximum(i * blocks_per_tile - 1, 0), off))
    row = lambda i: (i, 0)
    fixed = lambda i: (0, 0)
    return pl.pallas_call(
        _conv_outproj_ln_body,
        out_shape=(jax.ShapeDtypeStruct((m, d), F32),
                   jax.ShapeDtypeStruct((m, d), BF16)),
        grid=(m // tm,),
        in_specs=[pl.BlockSpec((tm, d), col(0)),
                  pl.BlockSpec((tm, d), col(1)),
                  pl.BlockSpec((tm, d), col(2)),
                  pl.BlockSpec((SUBLANES, d), prev(1)),
                  pl.BlockSpec((SUBLANES, d), prev(2)),
                  pl.BlockSpec((CONV_WIDTH, d), fixed),
                  pl.BlockSpec((d, d), fixed),
                  pl.BlockSpec((tm, d), row),
                  pl.BlockSpec((1, d), fixed),
                  pl.BlockSpec((1, d), fixed)],
        out_specs=(pl.BlockSpec((tm, d), row), pl.BlockSpec((tm, d), row)),
        compiler_params=_params(1),
        name="conv_outproj_ln",
    )(proj, proj, proj, proj, proj, conv_w, w, h, g.reshape(1, d), b.reshape(1, d))


def _tiles(s):
    return dict(
        proj_tm=min(1024, s), proj_tn=1024,
        out_tm=min(512, s), conv_tm=min(256, s),
        mlp_tm=min(1024, s), mlp_tf=512,
        hgrn_t=min(256, s), hgrn_heads=8,
        sb_tq=min(512, s), sb_tk=min(512, s), fox_tq=min(512, s), fox_tk=min(512, s),
        gate_t=min(256, s),
    )


def kernel(x, w_mix_a, norm_g_a, lb_logits, w_out_a, w_mix_b, w_out_b, w_mix_c, b_f_c,
           w_out_c, w_mix_d, conv_w_d, w_out_d, ln_mix_g, ln_mix_b, w_ff1, w_ff2,
           ln_ff_g, ln_ff_b):
    bsz, s, d = x.shape
    n_att_heads = d // ATT_HEAD_DIM
    tl = _tiles(s)
    w_ff1_b, w_ff2_b = w_ff1.astype(BF16), w_ff2.astype(BF16)
    outs = []
    for bi in range(bsz):
        h = x.reshape(s, d) if bsz == 1 else x[bi]
        for i in range(DEPTH):
            m, j = i % N_MIXERS, i // N_MIXERS
            g, b = ln_mix_g[i], ln_mix_b[i]
            if m == 0:
                proj = _proj(h, w_mix_a[j].astype(BF16), F32, tl["proj_tm"], tl["proj_tn"])
                y = _hgrn(proj, lb_logits, norm_g_a[j], i, tl["hgrn_t"], tl["hgrn_heads"])
                h, hb = _outproj_ln(y, w_out_a[j].astype(BF16), h, g, b, tl["out_tm"])
            elif m == 1:
                qkv = _proj(h, w_mix_b[j].astype(BF16), BF16, tl["proj_tm"], tl["proj_tn"])
                y = _stickbreak(qkv, n_att_heads, tl["sb_tq"], tl["sb_tk"])
                h, hb = _outproj_ln(y, w_out_b[j].astype(BF16), h, g, b, tl["out_tm"])
            elif m == 2:
                wc = w_mix_c[j].astype(BF16)
                qkv = _proj(h, wc, BF16, tl["proj_tm"], tl["proj_tn"], n=3 * d)
                lane_pad = HGRN_HEAD_DIM - n_att_heads
                wf = jnp.pad(wc[:, 3 * d:], ((0, 0), (0, lane_pad)))
                fg = _proj(h, wf, F32, tl["proj_tm"], tl["proj_tn"])
                bf = jnp.pad(b_f_c[j], (0, lane_pad)).reshape(1, -1)
                cum = _fox_gates(fg, bf, tl["gate_t"])[:, :n_att_heads]
                cum_t = cum.T
                y = _fox(qkv, cum_t[:, :, None], cum_t[:, None, :], n_att_heads,
                         tl["fox_tq"], tl["fox_tk"])
                h, hb = _outproj_ln(y, w_out_c[j].astype(BF16), h, g, b, tl["out_tm"])
            else:
                proj = _proj(h, w_mix_d[j].astype(BF16), F32, tl["proj_tm"], tl["proj_tn"])
                h, hb = _conv_outproj_ln(proj, conv_w_d[j], w_out_d[j].astype(BF16), h, g, b,
                                         tl["conv_tm"])
            h = _mlp_ln(hb, h, w_ff1_b, w_ff2_b, i,
                        ln_ff_g[i], ln_ff_b[i], tl["mlp_tm"], tl["mlp_tf"])
        outs.append(h)
    return outs[0][None] if bsz == 1 else jnp.stack(outs, axis=0)
```

```python
import functools

import jax
import jax.numpy as jnp
from jax import lax
from jax.experimental import pallas as pl
from jax.experimental.pallas import tpu as pltpu

DEPTH = 4
N_MIXERS = 4
HGRN_HEAD_DIM = 128
ATT_HEAD_DIM = 256
CHUNK = 16
CONV_WIDTH = 3
LN_EPS = 1e-5
RMS_EPS = 1e-6
ALPHA = (2.0 * DEPTH) ** 0.25
MASK_VALUE = -1e30
UNDERFLOW_MARGIN = 100.0
LOG2E = 1.4426950408889634
V7X_MXU_WIDTH = 256
OUTPROJ_ROW_STRIP = 128
SB_CUMSUM_WIDTH = V7X_MXU_WIDTH

V7X_VMEM_LIMIT_BYTES = 56 * 1024 * 1024
SUBLANES = 8
LANES = 128

F32 = jnp.float32
BF16 = jnp.bfloat16


def _params(n_grid_axes, vmem_bytes=V7X_VMEM_LIMIT_BYTES):
    return pltpu.CompilerParams(
        dimension_semantics=("arbitrary",) * n_grid_axes,
        vmem_limit_bytes=vmem_bytes)


def _layer_norm(v, g, b):
    mu = jnp.mean(v, axis=-1, keepdims=True)
    c = v - mu
    var = jnp.mean(c * c, axis=-1, keepdims=True)
    return c * lax.rsqrt(var + LN_EPS) * g + b


def _split3(x):
    hi = x.astype(BF16)
    r1 = x - hi.astype(F32)
    mid = r1.astype(BF16)
    lo = (r1 - mid.astype(F32)).astype(BF16)
    return hi, mid, lo


def _dot3(mat01, x):
    hi, mid, lo = _split3(x)
    out = jnp.dot(mat01, lo, preferred_element_type=F32)
    out += jnp.dot(mat01, mid, preferred_element_type=F32)
    out += jnp.dot(mat01, hi, preferred_element_type=F32)
    return out


def _softplus(z):
    return jnp.maximum(z, 0.0) + jnp.log1p(jnp.exp(-jnp.abs(z)))


def _proj_body(x_ref, w_ref, o_ref, xb_ref):
    @pl.when(pl.program_id(1) == 0)
    def _():
        xb_ref[...] = x_ref[...].astype(BF16)

    o_ref[...] = jnp.dot(xb_ref[...], w_ref[...],
                         preferred_element_type=F32).astype(o_ref.dtype)


def _proj(x, w, out_dtype, tm, tn, n=None):
    m, k = x.shape
    n = w.shape[1] if n is None else n
    tm, tn = min(tm, m), min(tn, n)
    assert m % tm == 0 and n % tn == 0, (m, n, tm, tn)
    return pl.pallas_call(
        _proj_body,
        out_shape=jax.ShapeDtypeStruct((m, n), out_dtype),
        grid=(m // tm, n // tn),
        in_specs=[pl.BlockSpec((tm, k), lambda i, j: (i, 0)),
                  pl.BlockSpec((k, tn), lambda i, j: (0, j))],
        out_specs=pl.BlockSpec((tm, tn), lambda i, j: (i, j)),
        scratch_shapes=[pltpu.VMEM((tm, k), BF16)],
        compiler_params=_params(2),
        name="proj",
    )(x, w)


def _outproj_ln_body(y_ref, w_ref, h_ref, g_ref, b_ref, of_ref, ob_ref):
    tm = y_ref.shape[0]
    strip = min(OUTPROJ_ROW_STRIP, tm)
    for r0 in range(0, tm, strip):
        rows = slice(r0, r0 + strip)
        acc = jnp.dot(y_ref[rows, :], w_ref[...], preferred_element_type=F32)
        out = _layer_norm(ALPHA * h_ref[rows, :] + acc, g_ref[...], b_ref[...])
        of_ref[rows, :] = out
        ob_ref[rows, :] = out.astype(BF16)


def _outproj_ln(y, w, h, g, b, tm):
    m, d = h.shape
    tm = min(tm, m)
    row = lambda i: (i, 0)
    fixed = lambda i: (0, 0)
    return pl.pallas_call(
        _outproj_ln_body,
        out_shape=(jax.ShapeDtypeStruct((m, d), F32),
                   jax.ShapeDtypeStruct((m, d), BF16)),
        grid=(m // tm,),
        in_specs=[pl.BlockSpec((tm, d), row),
                  pl.BlockSpec((d, d), fixed),
                  pl.BlockSpec((tm, d), row),
                  pl.BlockSpec((1, d), fixed),
                  pl.BlockSpec((1, d), fixed)],
        out_specs=(pl.BlockSpec((tm, d), row), pl.BlockSpec((tm, d), row)),
        compiler_params=_params(1),
        name="outproj_ln",
    )(y, w, h, g.reshape(1, d), b.reshape(1, d))


def _mlp_ln_body(xb_ref, h_ref, w1_ref, w2_ref, g_ref, b_ref, of_ref):
    j = pl.program_id(1)

    @pl.when(j == 0)
    def _():
        of_ref[...] = ALPHA * h_ref[...]

    a = jnp.dot(xb_ref[...], w1_ref[...], preferred_element_type=F32)
    a = jnp.maximum(a, 0.0)
    of_ref[...] += jnp.dot((a * a).astype(BF16), w2_ref[...], preferred_element_type=F32)

    @pl.when(j == pl.num_programs(1) - 1)
    def _():
        of_ref[...] = _layer_norm(of_ref[...], g_ref[...], b_ref[...])


def _mlp_ln(xb, h, w1, w2, layer, g, b, tm, tf):
    m, d = h.shape
    f = w1.shape[2]
    tm, tf = min(tm, m), min(tf, f)
    row = lambda i, j: (i, 0)
    fixed = lambda i, j: (0, 0)
    return pl.pallas_call(
        _mlp_ln_body,
        out_shape=jax.ShapeDtypeStruct((m, d), F32),
        grid=(m // tm, f // tf),
        in_specs=[pl.BlockSpec((tm, d), row),
                  pl.BlockSpec((tm, d), row),
                  pl.BlockSpec((None, d, tf), lambda i, j: (layer, 0, j)),
                  pl.BlockSpec((None, tf, d), lambda i, j: (layer, j, 0)),
                  pl.BlockSpec((1, d), fixed),
                  pl.BlockSpec((1, d), fixed)],
        out_specs=pl.BlockSpec((tm, d), row),
        compiler_params=_params(2),
        name="mlp_ln",
    )(xb, h, w1, w2, g.reshape(1, d), b.reshape(1, d))


def _hgrn_body(layer, n_heads, q_ref, f_ref, i_ref, g_ref, lbl_ref, ng_ref, o_ref,
               st_ref, qd_ref, kd_ref, v_ref, acc_ref):
    t = q_ref.shape[0]
    dk = HGRN_HEAD_DIM

    @pl.when(pl.program_id(1) == 0)
    def _():
        st_ref[...] = jnp.zeros_like(st_ref)

    lg = lbl_ref[...]
    e = jnp.exp(lg - jnp.max(lg, axis=0, keepdims=True))
    lb = jnp.sum(e[:layer + 1], axis=0, keepdims=True) / jnp.sum(e, axis=0, keepdims=True)

    f = f_ref[...]
    e = jnp.exp(-jnp.abs(f))
    log_sig = jnp.minimum(f, 0.0) - jnp.log(1.0 + e)
    a = jnp.log(lb)
    b = jnp.log1p(-lb) + log_sig
    log_f = jnp.maximum(a, b) + jnp.log(1.0 + jnp.exp(-jnp.abs(a - b)))
    k = (1.0 - lb) * (jnp.where(f >= 0.0, e, 1.0) / (1.0 + e))
    q = q_ref[...]
    q = q * jax.nn.sigmoid(q) * dk ** -0.5

    r = lax.broadcasted_iota(jnp.int32, (t, t), 0)
    c = lax.broadcasted_iota(jnp.int32, (t, t), 1)
    same_chunk = (r // CHUNK) == (c // CHUNK)
    causal = same_chunk & (c <= r)
    tri = jnp.where(causal, 1.0, 0.0).astype(BF16)
    ones = jnp.where(same_chunk, 1.0, 0.0).astype(BF16)
    bc = _dot3(tri, log_f)
    b_last = _dot3(ones, log_f)

    qd_ref[...] = (q * jnp.exp(bc)).astype(BF16)
    k_inv = (k * jnp.exp(-bc)).astype(BF16)
    kd_ref[...] = (k * jnp.exp(b_last - bc)).astype(BF16)
    v_ref[...] = i_ref[...].astype(BF16)

    chunk_onehot = jnp.where(
        lax.broadcasted_iota(jnp.int32, (t, LANES), 0) // CHUNK
        == lax.broadcasted_iota(jnp.int32, (t, LANES), 1), 1.0, 0.0).astype(BF16)
    contract_rows = (((0,), (0,)), ((), ()))
    decay_cols = jnp.exp(sum(
        lax.dot_general(part, chunk_onehot, contract_rows, preferred_element_type=F32)
        for part in reversed(_split3(log_f))))

    for h in range(n_heads):
        cols = slice(h * dk, (h + 1) * dk)
        sc = lax.dot_general(qd_ref[:, cols], k_inv[:, cols], (((1,), (1,)), ((), ())),
                             preferred_element_type=F32)
        sc = jnp.where(causal, sc, 0.0).astype(BF16)
        acc_ref[:, cols] = jnp.dot(sc, v_ref[:, cols], preferred_element_type=F32)

    n_chunks = t // CHUNK
    chunk_of_row = lax.broadcasted_iota(jnp.int32, (t, dk), 0) // CHUNK
    for h in range(n_heads):
        cols = slice(h * dk, (h + 1) * dk)
        v = i_ref[:, cols]
        v_by_chunk = jnp.concatenate(
            [jnp.where(chunk_of_row == ci, v, 0.0) for ci in range(n_chunks)], axis=1)
        upd = lax.dot_general(kd_ref[:, cols], v_by_chunk.astype(BF16), contract_rows,
                              preferred_element_type=F32)
        st = st_ref[h]
        for ci in range(n_chunks):
            rows = slice(ci * CHUNK, (ci + 1) * CHUNK)
            acc_ref[rows, cols] += jnp.dot(qd_ref[rows, cols], st.astype(BF16),
                                           preferred_element_type=F32)
            st = decay_cols[h * dk:(h + 1) * dk, ci:ci + 1] * st + upd[:, ci * dk:(ci + 1) * dk]
        st_ref[h] = st

    ng = ng_ref[...]
    g = g_ref[...]
    gate = g * jax.nn.sigmoid(g)
    for h in range(n_heads):
        cols = slice(h * dk, (h + 1) * dk)
        o = acc_ref[:, cols]
        o = o * lax.rsqrt(jnp.mean(o * o, axis=-1, keepdims=True) + RMS_EPS) * ng
        o_ref[:, cols] = (o * gate[:, cols]).astype(BF16)


def _hgrn(proj, lb_logits, norm_g, layer, t_block, heads_per_step):
    s, d4 = proj.shape
    d = d4 // 4
    t_block = min(t_block, s)
    gw = heads_per_step * HGRN_HEAD_DIM
    n_groups = d // gw
    n_lb = lb_logits.shape[0]
    col = lambda off: (lambda hg, tb: (tb, off * n_groups + hg))
    return pl.pallas_call(
        functools.partial(_hgrn_body, layer, heads_per_step),
        out_shape=jax.ShapeDtypeStruct((s, d), BF16),
        grid=(n_groups, s // t_block),
        in_specs=[pl.BlockSpec((t_block, gw), col(0)),
                  pl.BlockSpec((t_block, gw), col(1)),
                  pl.BlockSpec((t_block, gw), col(2)),
                  pl.BlockSpec((t_block, gw), col(3)),
                  pl.BlockSpec((n_lb, gw), lambda hg, tb: (0, hg)),
                  pl.BlockSpec((1, HGRN_HEAD_DIM), lambda hg, tb: (0, 0))],
        out_specs=pl.BlockSpec((t_block, gw), lambda hg, tb: (tb, hg)),
        scratch_shapes=[pltpu.VMEM((heads_per_step, HGRN_HEAD_DIM, HGRN_HEAD_DIM), F32),
                        pltpu.VMEM((t_block, gw), BF16),
                        pltpu.VMEM((t_block, gw), BF16),
                        pltpu.VMEM((t_block, gw), BF16),
                        pltpu.VMEM((t_block, gw), F32)],
        compiler_params=_params(2),
        name="hgrn2",
    )(proj, proj, proj, proj, lb_logits, norm_g.reshape(1, HGRN_HEAD_DIM))


def _max_sq_row_norm(ref, chunk):
    def body(c, best):
        x = ref[pl.ds(pl.multiple_of(c * chunk, chunk), chunk), :].astype(F32)
        sq = jnp.sum(x * x, axis=-1, keepdims=True)
        return jnp.maximum(best, jnp.max(sq, axis=0, keepdims=True))
    return lax.fori_loop(0, ref.shape[0] // chunk, body, jnp.zeros((1, 1), F32))


def _stickbreak_body(q_ref, k_ref, v_ref, o_ref, acc_ref, lw_ref, knorm_ref):
    tq, d = q_ref.shape
    tk = lw_ref.shape[1]
    cw = min(SB_CUMSUM_WIDTH, tk)
    first_row = pl.program_id(1) * tq
    diag = first_row // tk
    q = q_ref[...] * d ** -0.5
    r = lax.broadcasted_iota(jnp.int32, (cw, cw), 0)
    c = lax.broadcasted_iota(jnp.int32, (cw, cw), 1)
    incl_rev = jnp.where(r >= c, 1.0, 0.0).astype(BF16)

    def log_weights(j, diagonal):
        rows = pl.ds(pl.multiple_of(j * tk, tk), tk)
        z = lax.dot_general(q, k_ref[rows, :], (((1,), (1,)), ((), ())),
                            preferred_element_type=F32)
        if diagonal:
            ahead = (lax.broadcasted_iota(jnp.int32, (tq, tk), 1)
                     - lax.broadcasted_iota(jnp.int32, (tq, tk), 0))
            z = jnp.where(ahead < first_row - j * tk, z, MASK_VALUE)
        sp = jnp.maximum(z, 0.0) + jnp.log(1.0 + jnp.exp2(jnp.abs(z) * -LOG2E))
        sp = sp.astype(BF16)
        later = None
        for sub in reversed(range(tk // cw)):
            cols = slice(sub * cw, (sub + 1) * cw)
            within = jnp.dot(sp[:, cols], incl_rev, preferred_element_type=F32)
            total = within if later is None else within + later
            lw_ref[:, cols] = z[:, cols] - total
            later = total[:, 0:1]
        return later

    def accumulate(j, offs):
        rows = pl.ds(pl.multiple_of(j * tk, tk), tk)
        w = jnp.exp(lw_ref[...] - offs)
        acc_ref[...] += jnp.dot(w.astype(BF16), v_ref[rows, :], preferred_element_type=F32)

    @pl.when(pl.program_id(1) == 0)
    def _():
        knorm_ref[...] = jnp.broadcast_to(_max_sq_row_norm(k_ref, tk), knorm_ref.shape)

    qf = q.astype(F32)
    q_sq = jnp.max(jnp.sum(qf * qf, axis=-1, keepdims=True), axis=0, keepdims=True)
    slack = jnp.sqrt(q_sq * knorm_ref[0:1, 0:1]) * 2.0 ** -8

    def earlier_blocks_matter(offs):
        return (jnp.min(offs - slack) <= UNDERFLOW_MARGIN).astype(jnp.int32)

    def wanted(carry):
        n, _, _, more = carry
        return jnp.logical_and(n < diag, more == 1)

    def step(carry):
        n, offs, block_sum, _ = carry
        j = diag - n
        accumulate(j, offs)
        offs = offs + block_sum
        block_sum = log_weights(j - 1, False)
        return n + 1, offs, block_sum, earlier_blocks_matter(offs + block_sum)

    acc_ref[...] = jnp.zeros_like(acc_ref)
    block_sum = log_weights(diag, True)
    carry = (jnp.int32(0), jnp.zeros((tq, 1), F32), block_sum, earlier_blocks_matter(block_sum))
    n, offs, _, _ = lax.while_loop(wanted, step, carry)
    accumulate(diag - n, offs)
    o_ref[...] = acc_ref[...].astype(o_ref.dtype)


def _stickbreak(qkv, n_heads, tq, tk):
    s = qkv.shape[0]
    d = ATT_HEAD_DIM
    tq, tk = min(tq, s), min(tk, s)
    assert s % tk == 0 and tk % tq == 0, (s, tq, tk)
    return pl.pallas_call(
        _stickbreak_body,
        out_shape=jax.ShapeDtypeStruct((s, n_heads * d), BF16),
        grid=(n_heads, s // tq),
        in_specs=[pl.BlockSpec((tq, d), lambda h, i: (i, h)),
                  pl.BlockSpec((s, d), lambda h, i: (0, n_heads + h)),
                  pl.BlockSpec((s, d), lambda h, i: (0, 2 * n_heads + h))],
        out_specs=pl.BlockSpec((tq, d), lambda h, i: (i, h)),
        scratch_shapes=[pltpu.VMEM((tq, d), F32),
                        pltpu.VMEM((tq, tk), F32),
                        pltpu.VMEM((SUBLANES, LANES), F32)],
        compiler_params=_params(2),
        name="stickbreak_attn",
    )(qkv, qkv, qkv)


def _fox_gate_body(h_ref, wf_ref, bf_ref, cum_ref, carry_ref):
    t = h_ref.shape[0]

    @pl.when(pl.program_id(0) == 0)
    def _():
        carry_ref[...] = jnp.zeros_like(carry_ref)

    x = jnp.dot(h_ref[...].astype(BF16), wf_ref[...], preferred_element_type=F32) + bf_ref[...]
    log_f = jnp.minimum(x, 0.0) - jnp.log1p(jnp.exp(-jnp.abs(x)))
    r = lax.broadcasted_iota(jnp.int32, (t, t), 0)
    c = lax.broadcasted_iota(jnp.int32, (t, t), 1)
    tri = jnp.where(c <= r, 1.0, 0.0).astype(BF16)
    cum = _dot3(tri, log_f) + carry_ref[...]
    cum_ref[...] = cum
    carry_ref[...] = cum[t - 1:t, :]


def _fox_gates(h, wf, bf, t_block):
    s, d = h.shape
    w = wf.shape[1]
    t_block = min(t_block, s)
    assert s % t_block == 0
    return pl.pallas_call(
        _fox_gate_body,
        out_shape=jax.ShapeDtypeStruct((s, w), F32),
        grid=(s // t_block,),
        in_specs=[pl.BlockSpec((t_block, d), lambda i: (i, 0)),
                  pl.BlockSpec((d, w), lambda i: (0, 0)),
                  pl.BlockSpec((1, w), lambda i: (0, 0))],
        out_specs=pl.BlockSpec((t_block, w), lambda i: (i, 0)),
        scratch_shapes=[pltpu.VMEM((1, w), F32)],
        compiler_params=_params(1),
        name="fox_gates",
    )(h, wf, bf)


def _fox_body(q_ref, k_ref, v_ref, cq_ref, ck_ref, o_ref, acc_ref, z_ref, knorm_ref):
    tq, d = q_ref.shape
    tk = z_ref.shape[1]
    first_row = pl.program_id(1) * tq
    diag = first_row // tk
    q = q_ref[...] * d ** -0.5
    cq = cq_ref[...]

    def logits(j, diagonal):
        rows = pl.ds(pl.multiple_of(j * tk, tk), tk)
        z = lax.dot_general(q, k_ref[rows, :], (((1,), (1,)), ((), ())),
                            preferred_element_type=F32)
        z = (z + cq) - ck_ref[:, rows]
        if diagonal:
            ahead = (lax.broadcasted_iota(jnp.int32, (tq, tk), 1)
                     - lax.broadcasted_iota(jnp.int32, (tq, tk), 0))
            z = jnp.where(ahead <= first_row - j * tk, z, MASK_VALUE)
        z_ref[...] = z
        return jnp.max(z, axis=-1, keepdims=True)

    def accumulate(j, m, l, row_max):
        rows = pl.ds(pl.multiple_of(j * tk, tk), tk)
        m_new = jnp.maximum(m, row_max)
        scale = jnp.exp(m - m_new)
        p = jnp.exp(z_ref[...] - m_new)
        l = scale * l + jnp.sum(p, axis=-1, keepdims=True)
        acc_ref[...] = scale * acc_ref[...] + jnp.dot(
            p.astype(BF16), v_ref[rows, :], preferred_element_type=F32)
        return m_new, l

    @pl.when(pl.program_id(1) == 0)
    def _():
        knorm_ref[...] = jnp.broadcast_to(_max_sq_row_norm(k_ref, tk), knorm_ref.shape)

    qf = q.astype(F32)
    q_sq = jnp.sum(qf * qf, axis=-1, keepdims=True)
    reach = jnp.sqrt(q_sq * knorm_ref[0:1, 0:1]) + cq

    def blocks_up_to_matter(last_block, m):
        last_block = jnp.maximum(last_block, 0)
        ck_end = jnp.min(ck_ref[:, pl.ds(pl.multiple_of((last_block + 1) * tk - LANES, LANES),
                                         LANES)])
        return (jnp.min(m - reach) + ck_end <= UNDERFLOW_MARGIN).astype(jnp.int32)

    def wanted(carry):
        n, _, _, _, more = carry
        return jnp.logical_and(n < diag, more == 1)

    def step(carry):
        n, m, l, row_max, _ = carry
        j = diag - n
        m, l = accumulate(j, m, l, row_max)
        row_max = logits(j - 1, False)
        return n + 1, m, l, row_max, blocks_up_to_matter(j - 2, jnp.maximum(m, row_max))

    acc_ref[...] = jnp.zeros_like(acc_ref)
    row_max = logits(diag, True)
    carry = (jnp.int32(0), jnp.full((tq, 1), MASK_VALUE, F32), jnp.zeros((tq, 1), F32),
             row_max, blocks_up_to_matter(diag - 1, row_max))
    n, m, l, row_max, _ = lax.while_loop(wanted, step, carry)
    _, l = accumulate(diag - n, m, l, row_max)
    o_ref[...] = (acc_ref[...] / l).astype(o_ref.dtype)


def _fox(qkv, cum_col, cum_row, n_heads, tq, tk):
    s = qkv.shape[0]
    d = ATT_HEAD_DIM
    tq, tk = min(tq, s), min(tk, s)
    assert s % tk == 0 and tk % tq == 0, (s, tq, tk)
    return pl.pallas_call(
        _fox_body,
        out_shape=jax.ShapeDtypeStruct((s, n_heads * d), BF16),
        grid=(n_heads, s // tq),
        in_specs=[pl.BlockSpec((tq, d), lambda h, i: (i, h)),
                  pl.BlockSpec((s, d), lambda h, i: (0, n_heads + h)),
                  pl.BlockSpec((s, d), lambda h, i: (0, 2 * n_heads + h)),
                  pl.BlockSpec((None, tq, 1), lambda h, i: (h, i, 0)),
                  pl.BlockSpec((None, 1, s), lambda h, i: (h, 0, 0))],
        out_specs=pl.BlockSpec((tq, d), lambda h, i: (i, h)),
        scratch_shapes=[pltpu.VMEM((tq, d), F32),
                        pltpu.VMEM((tq, tk), F32),
                        pltpu.VMEM((SUBLANES, LANES), F32)],
        compiler_params=_params(2),
        name="fox_attn",
    )(qkv, qkv, qkv, cum_col, cum_row)


def _conv_outproj_ln_body(bg_ref, cg_ref, hd_ref, cgp_ref, hdp_ref, cw_ref, w_ref, h_ref,
                          g_ref, b_ref, of_ref, ob_ref):
    i = pl.program_id(0)
    u = cg_ref[...] * hd_ref[...]
    u_prev = jnp.where(i > 0, cgp_ref[...] * hdp_ref[...], 0.0)
    row8 = lax.broadcasted_iota(jnp.int32, u_prev.shape, 0)
    y = cw_ref[CONV_WIDTH - 1:CONV_WIDTH, :] * u
    for lag in range(1, CONV_WIDTH):
        shifted = pltpu.roll(u, lag, 0)
        head = jnp.where(row8 < lag, pltpu.roll(u_prev, lag, 0), shifted[:SUBLANES])
        shifted = jnp.concatenate([head, shifted[SUBLANES:]], axis=0)
        y += cw_ref[CONV_WIDTH - 1 - lag:CONV_WIDTH - lag, :] * shifted
    acc = jnp.dot((bg_ref[...] * y).astype(BF16), w_ref[...], preferred_element_type=F32)
    out = _layer_norm(ALPHA * h_ref[...] + acc, g_ref[...], b_ref[...])
    of_ref[...] = out
    ob_ref[...] = out.astype(BF16)


def _conv_outproj_ln(proj, conv_w, w, h, g, b, tm):
    m, d = h.shape
    tm = min(tm, m)
    blocks_per_tile = tm // SUBLANES
    col = lambda off: (lambda i: (i, off))
    prev = lambda off: (lambda i: (jnp.maximum(i * blocks_per_tile - 1, 0), off))
    row = lambda i: (i, 0)
    fixed = lambda i: (0, 0)
    return pl.pallas_call(
        _conv_outproj_ln_body,
        out_shape=(jax.ShapeDtypeStruct((m, d), F32),
                   jax.ShapeDtypeStruct((m, d), BF16)),
        grid=(m // tm,),
        in_specs=[pl.BlockSpec((tm, d), col(0)),
                  pl.BlockSpec((tm, d), col(1)),
                  pl.BlockSpec((tm, d), col(2)),
                  pl.BlockSpec((SUBLANES, d), prev(1)),
                  pl.BlockSpec((SUBLANES, d), prev(2)),
                  pl.BlockSpec((CONV_WIDTH, d), fixed),
                  pl.BlockSpec((d, d), fixed),
                  pl.BlockSpec((tm, d), row),
                  pl.BlockSpec((1, d), fixed),
                  pl.BlockSpec((1, d), fixed)],
        out_specs=(pl.BlockSpec((tm, d), row), pl.BlockSpec((tm, d), row)),
        compiler_params=_params(1),
        name="conv_outproj_ln",
    )(proj, proj, proj, proj, proj, conv_w, w, h, g.reshape(1, d), b.reshape(1, d))


def _tiles(s):
    return dict(
        proj_tm=min(1024, s), proj_tn=1024,
        out_tm=min(512, s), conv_tm=min(256, s),
        mlp_tm=min(1024, s), mlp_tf=512,
        hgrn_t=min(256, s), hgrn_heads=8,
        sb_tq=min(512, s), sb_tk=min(512, s), fox_tq=min(512, s), fox_tk=min(512, s),
        gate_t=min(512, s),
    )


def kernel(x, w_mix_a, norm_g_a, lb_logits, w_out_a, w_mix_b, w_out_b, w_mix_c, b_f_c,
           w_out_c, w_mix_d, conv_w_d, w_out_d, ln_mix_g, ln_mix_b, w_ff1, w_ff2,
           ln_ff_g, ln_ff_b):
    bsz, s, d = x.shape
    n_att_heads = d // ATT_HEAD_DIM
    tl = _tiles(s)
    w_ff1_b, w_ff2_b = w_ff1.astype(BF16), w_ff2.astype(BF16)
    outs = []
    for bi in range(bsz):
        h = x.reshape(s, d) if bsz == 1 else x[bi]
        for i in range(DEPTH):
            m, j = i % N_MIXERS, i // N_MIXERS
            g, b = ln_mix_g[i], ln_mix_b[i]
            if m == 0:
                proj = _proj(h, w_mix_a[j].astype(BF16), F32, tl["proj_tm"], tl["proj_tn"])
                y = _hgrn(proj, lb_logits, norm_g_a[j], i, tl["hgrn_t"], tl["hgrn_heads"])
                h, hb = _outproj_ln(y, w_out_a[j].astype(BF16), h, g, b, tl["out_tm"])
            elif m == 1:
                qkv = _proj(h, w_mix_b[j].astype(BF16), BF16, tl["proj_tm"], tl["proj_tn"])
                y = _stickbreak(qkv, n_att_heads, tl["sb_tq"], tl["sb_tk"])
                h, hb = _outproj_ln(y, w_out_b[j].astype(BF16), h, g, b, tl["out_tm"])
            elif m == 2:
                wc = w_mix_c[j].astype(BF16)
                qkv = _proj(h, wc, BF16, tl["proj_tm"], tl["proj_tn"], n=3 * d)
                lane_pad = HGRN_HEAD_DIM - n_att_heads
                wf = jnp.pad(wc[:, 3 * d:], ((0, 0), (0, lane_pad)))
                bf = jnp.pad(b_f_c[j], (0, lane_pad)).reshape(1, -1)
                cum = _fox_gates(h, wf, bf, tl["gate_t"])[:, :n_att_heads]
                cum_t = cum.T
                y = _fox(qkv, cum_t[:, :, None], cum_t[:, None, :], n_att_heads,
                         tl["fox_tq"], tl["fox_tk"])
                h, hb = _outproj_ln(y, w_out_c[j].astype(BF16), h, g, b, tl["out_tm"])
            else:
                proj = _proj(h, w_mix_d[j].astype(BF16), F32, tl["proj_tm"], tl["proj_tn"])
                h, hb = _conv_outproj_ln(proj, conv_w_d[j], w_out_d[j].astype(BF16), h, g, b,
                                         tl["conv_tm"])
            h = _mlp_ln(hb, h, w_ff1_b, w_ff2_b, i,
                        ln_ff_g[i], ln_ff_b[i], tl["mlp_tm"], tl["mlp_tf"])
        outs.append(h)
    return outs[0][None] if bsz == 1 else jnp.stack(outs, axis=0)
```
